```python
import math
import jax
import jax.numpy as jnp
from jax import lax
import numpy as np

D_MODEL = 1024
BATCH = 32
SEQ = 2048
DEPTH = 4

HG_HEADS = 4
HG_DK = 128
HG_DV = 128
HG_F = HG_HEADS * HG_DK
HG_V = HG_HEADS * HG_DV
HG_CHUNK = 64
SB_HEADS = 8
SB_DH = 64
SB_W = SB_HEADS * SB_DH
SB_BLOCK = 128
PEER_HEADS = 8
PEER_NKEYS = 128
PEER_N = PEER_NKEYS * PEER_NKEYS
PEER_DKEY = 128
PEER_TOPK = 16
PEER_TOK_BLOCK = 16
DN_ALPHA = (2.0 * DEPTH) ** 0.25
DN_BETA = (8.0 * DEPTH) ** -0.25
LN_EPS = 1e-5
RMS_EPS = 1e-6
IN_SIZES = (HG_F, HG_F, HG_V, HG_V, SB_W, SB_W, SB_W, D_MODEL, D_MODEL)
IN_WIDTH = sum(IN_SIZES)
IN_SPLITS = tuple(int(s) for s in np.cumsum(IN_SIZES)[:-1])

kernel_name = "hybrid_hgrn2_stickbreak_peer_deepnorm"


def layer_norm(x, g, b):
    xf = x.astype(jnp.float32)
    mu = jnp.mean(xf, axis=-1, keepdims=True)
    var = jnp.mean(jnp.square(xf - mu), axis=-1, keepdims=True)
    return (xf - mu) * lax.rsqrt(var + LN_EPS) * g + b


def head_rms_norm(o, g):
    of = o.astype(jnp.float32)
    return of * lax.rsqrt(jnp.mean(jnp.square(of), axis=-1, keepdims=True) + RMS_EPS) * g


def hgrn2_chunkwise(q, f_logit, v, lb):
    B, S, H, DK = q.shape
    DV = v.shape[-1]
    C = HG_CHUNK
    NC = S // C
    q = q.astype(jnp.float32)
    z = f_logit.astype(jnp.float32)
    v = v.astype(jnp.float32)
    lb = lb.reshape(H, DK)
    log_f = jnp.log(lb + (1.0 - lb) * jax.nn.sigmoid(z))
    k = (1.0 - lb) * jax.nn.sigmoid(-z)

    def chunks(a):
        return a.reshape(B, NC, C, H, a.shape[-1]).transpose(1, 0, 3, 2, 4)

    qc, kc, vc, gc = chunks(q), chunks(k), chunks(v), chunks(log_f)
    bc = jnp.cumsum(gc, axis=3)
    tri = jnp.tril(jnp.ones((C, C), dtype=bool))[:, :, None]

    def step(state, inp):
        q_, k_, v_, b_ = inp
        b_end = b_[:, :, -1:, :]
        inter = jnp.einsum('bhtd,bhde->bhte', q_ * jnp.exp(b_), state)
        diff = b_[:, :, :, None, :] - b_[:, :, None, :, :]
        decay = jnp.where(tri, jnp.exp(jnp.where(tri, diff, 0.0)), 0.0)
        scores = jnp.einsum('bhtd,bhsd,bhtsd->bhts', q_, k_, decay)
        intra = jnp.einsum('bhts,bhse->bhte', scores, v_)
        new_state = (jnp.exp(b_end[:, :, 0, :])[..., None] * state
                     + jnp.einsum('bhsd,bhse->bhde', k_ * jnp.exp(b_end - b_), v_))
        return new_state, inter + intra

    state0 = jnp.zeros((B, H, DK, DV), jnp.float32)
    _, o = lax.scan(step, state0, (qc, kc, vc, bc))
    return o.transpose(1, 0, 3, 2, 4).reshape(B, S, H, DV)


def stick_breaking_attention(q, k, v):
    B, S, H, d = q.shape
    scale = 1.0 / math.sqrt(d)
    qh = q.astype(jnp.float32).transpose(0, 2, 1, 3)
    kh = k.astype(jnp.float32).transpose(0, 2, 1, 3)
    vh = v.astype(jnp.float32).transpose(0, 2, 1, 3)
    outs = []
    for blk in range(S // SB_BLOCK):
        t0 = blk * SB_BLOCK
        t1 = t0 + SB_BLOCK
        z = jnp.einsum('bhtd,bhsd->bhts', qh[:, :, t0:t1], kh[:, :, :t1]) * scale
        tpos = t0 + jnp.arange(SB_BLOCK)[:, None]
        spos = jnp.arange(t1)[None, :]
        causal = spos < tpos
        log_1mb = jnp.where(causal, jax.nn.log_sigmoid(-z), 0.0)
        after = lax.cumsum(log_1mb, axis=3, reverse=True) - log_1mb
        w = jnp.where(causal, jnp.exp(jax.nn.log_sigmoid(z) + after), 0.0)
        outs.append(jnp.einsum('bhts,bhsd->bhtd', w, vh[:, :, :t1]))
    o = jnp.concatenate(outs, axis=2)
    return o.transpose(0, 2, 1, 3)


def peer_ffn(h, w_pq, sub_keys, peer_u, peer_v):
    B, S, D = h.shape
    half = PEER_DKEY // 2
    q = (h @ w_pq).reshape(B, S, PEER_HEADS, 2, half).astype(jnp.float32)
    s1 = jnp.einsum('bshd,hnd->bshn', q[..., 0, :], sub_keys[0].astype(jnp.float32))
    s2 = jnp.einsum('bshd,hnd->bshn', q[..., 1, :], sub_keys[1].astype(jnp.float32))
    v1, i1 = lax.top_k(s1, PEER_TOPK)
    v2, i2 = lax.top_k(s2, PEER_TOPK)
    cand = (v1[..., :, None] + v2[..., None, :]).reshape(B, S, PEER_HEADS, PEER_TOPK * PEER_TOPK)
    cand_idx = (i1[..., :, None] * PEER_NKEYS + i2[..., None, :]).reshape(B, S, PEER_HEADS, PEER_TOPK * PEER_TOPK)
    top, pos = lax.top_k(cand, PEER_TOPK)
    idx = jnp.take_along_axis(cand_idx, pos, axis=-1)
    gates = jax.nn.softmax(top, axis=-1)

    T = PEER_TOK_BLOCK
    nb = S // T

    def blockify(a):
        return a.reshape((B, nb, T) + a.shape[2:]).swapaxes(0, 1)

    def expert_block(args):
        hb, ib, gb = args
        u = peer_u[ib]
        act = jax.nn.gelu(jnp.einsum('btd,bthkd->bthk', hb, u), approximate=False)
        return jnp.einsum('bthk,bthkd->btd', gb * act, peer_v[ib])

    out = lax.map(expert_block, (blockify(h), blockify(idx), blockify(gates)))
    return out.swapaxes(0, 1).reshape(B, S, D)


def setup_inputs(seed: int = 0) -> dict:
    key = jax.random.key(seed)
    ks = jax.random.split(key, 16)
    nrm = jax.random.normal
    f32 = jnp.float32
    D = D_MODEL
    return {
        "x": nrm(ks[0], (BATCH, SEQ, D), f32),
        "c": nrm(ks[1], (BATCH, D), f32),
        "w_ada": nrm(ks[2], (DEPTH, D, 6 * D), f32) * (0.5 * D ** -0.5),
        "b_ada": nrm(ks[3], (DEPTH, 6 * D), f32) * 0.01,
        "w_in": nrm(ks[4], (DEPTH, D, IN_WIDTH), f32) * D ** -0.5,
        "lb_logits": nrm(ks[5], (DEPTH, HG_F), f32) * 0.1,
        "hg_norm_g": 1.0 + 0.02 * nrm(ks[6], (DEPTH, HG_V), f32),
        "w_up_a": nrm(ks[7], (DEPTH, HG_V, D), f32) * HG_V ** -0.5,
        "w_up_b": nrm(ks[8], (DEPTH, SB_W, D), f32) * SB_W ** -0.5,
        "w_o": nrm(ks[9], (DEPTH, D, D), f32) * (DN_BETA * D ** -0.5),
        "w_pq": nrm(ks[10], (DEPTH, D, PEER_HEADS * PEER_DKEY), f32) * D ** -0.5,
        "sub_keys": nrm(ks[11], (DEPTH, 2, PEER_HEADS, PEER_NKEYS, PEER_DKEY // 2), f32) * (PEER_DKEY // 2) ** -0.5,
        "peer_u": nrm(ks[12], (DEPTH, PEER_N, D), f32) * D ** -0.5,
        "peer_v": nrm(ks[13], (DEPTH, PEER_N, D), f32) * DN_BETA,
        "ln_g": 1.0 + 0.02 * nrm(ks[14], (DEPTH, 2, D), f32),
        "ln_b": 0.02 * nrm(ks[15], (DEPTH, 2, D), f32),
    }


def reference(x, c, w_ada, b_ada, w_in, lb_logits, hg_norm_g, w_up_a, w_up_b, w_o,
              w_pq, sub_keys, peer_u, peer_v, ln_g, ln_b):
    B, S, _ = x.shape
    p = jax.nn.softmax(lb_logits.astype(jnp.float32), axis=0)
    lower_bounds = jnp.cumsum(p, axis=0) - p[0:1]
    cond = jax.nn.silu(c)
    for l in range(DEPTH):
        mod = cond @ w_ada[l] + b_ada[l]
        sh1, sc1, gt1, sh2, sc2, gt2 = jnp.split(mod, 6, axis=-1)

        hm = x * (1.0 + sc1[:, None]) + sh1[:, None]
        proj = hm @ w_in[l]
        qa, fa, ia, ga, qb, kb, vb, gate_a, gate_b = jnp.split(proj, IN_SPLITS, axis=-1)
        oa = hgrn2_chunkwise(qa.reshape(B, S, HG_HEADS, HG_DK),
                             fa.reshape(B, S, HG_HEADS, HG_DK),
                             ia.reshape(B, S, HG_HEADS, HG_DV),
                             lower_bounds[l])
        oa = head_rms_norm(oa, hg_norm_g[l].reshape(HG_HEADS, HG_DV)).reshape(B, S, HG_V) * jax.nn.silu(ga)
        ob = stick_breaking_attention(qb.reshape(B, S, SB_HEADS, SB_DH),
                                      kb.reshape(B, S, SB_HEADS, SB_DH),
                                      vb.reshape(B, S, SB_HEADS, SB_DH)).reshape(B, S, SB_W)
        merged = jax.nn.sigmoid(gate_a) * (oa @ w_up_a[l]) + jax.nn.sigmoid(gate_b) * (ob @ w_up_b[l])
        y = (merged @ w_o[l]) * gt1[:, None]
        x = layer_norm(DN_ALPHA * x + y, ln_g[l, 0], ln_b[l, 0])

        hf = x * (1.0 + sc2[:, None]) + sh2[:, None]
        y = peer_ffn(hf, w_pq[l], sub_keys[l], peer_u[l], peer_v[l]) * gt2[:, None]
        x = layer_norm(DN_ALPHA * x + y, ln_g[l, 1], ln_b[l, 1])
    return x
```

```python
import functools
import math

import jax
import jax.numpy as jnp
from jax import lax
from jax.experimental import pallas as pl
from jax.experimental.pallas import tpu as pltpu

f32 = jnp.float32
bf16 = jnp.bfloat16
i32 = jnp.int32

D_MODEL = 1024
DEPTH = 4
HG_HEADS = 4
HG_D = 128
HG_W = HG_HEADS * HG_D
SB_HEADS = 8
SB_DH = 64
SB_W = SB_HEADS * SB_DH
SB_BLOCK = 128
PEER_HEADS = 8
PEER_NKEYS = 128
PEER_N = PEER_NKEYS * PEER_NKEYS
PEER_DKEY = 128
PEER_TOPK = 16
PEER_PAIRS = PEER_HEADS * PEER_TOPK
IN_WIDTH = 4 * HG_W + 3 * SB_W + 2 * D_MODEL
DN_ALPHA = (2.0 * DEPTH) ** 0.25
LN_EPS = 1e-5
RMS_EPS = 1e-6

LANES = 128
SUBLANES = 8
ROW_WORDS = D_MODEL // 2
ROW_SUB = ROW_WORDS // LANES
VMEM_LIMIT = 48 * 1024 * 1024

COL_QA, COL_FA, COL_IA = 0, 4, 8
COL_QB, COL_KB, COL_VB = 16, 20, 24
COL512_GA, COL512_GATE_A, COL512_GATE_B = 3, 7, 9


def _nt(a, b):
    return lax.dot_general(a, b, (((1,), (1,)), ((), ())), preferred_element_type=f32)


def _tn(a, b):
    return lax.dot_general(a, b, (((0,), (0,)), ((), ())), preferred_element_type=f32)


def _dot(a, b):
    return jnp.dot(a, b, preferred_element_type=f32)


def _split_bf16(a):
    hi = a.astype(bf16)
    lo = (a - hi.astype(f32)).astype(bf16)
    return hi, lo


def _params(sem):
    return pltpu.CompilerParams(dimension_semantics=sem, vmem_limit_bytes=VMEM_LIMIT)


def _mod_kernel(c_ref, w_ref, b_ref, o_ref):
    c = c_ref[...]
    cond = (c * jax.nn.sigmoid(c)).astype(bf16)
    o_ref[...] = _dot(cond, w_ref[...].astype(bf16)) + b_ref[...]


def _mod_call(c, w_ada_l, b_ada_l):
    B = c.shape[0]
    tn = 1536
    return pl.pallas_call(
        _mod_kernel,
        grid=(6 * D_MODEL // tn,),
        in_specs=[pl.BlockSpec((B, D_MODEL), lambda j: (0, 0)),
                  pl.BlockSpec((D_MODEL, tn), lambda j: (0, j)),
                  pl.BlockSpec((1, tn), lambda j: (0, j))],
        out_specs=pl.BlockSpec((B, tn), lambda j: (0, j)),
        out_shape=jax.ShapeDtypeStruct((B, 6 * D_MODEL), f32),
        compiler_params=_params(("arbitrary",)),
        name="mod",
    )(c, w_ada_l, b_ada_l)


def _mod_spec(which, tiles_per_seq):
    return pl.BlockSpec((None, None, 1, D_MODEL), lambda i: (i // tiles_per_seq, which, 0, 0))


def _proj_kernel(x_ref, sh_ref, sc_ref, w_ref, o_ref):
    hm = (x_ref[...] * (1.0 + sc_ref[...]) + sh_ref[...]).astype(bf16)
    for n0 in range(0, IN_WIDTH, 512):
        o_ref[:, n0:n0 + 512] = _dot(hm, w_ref[:, n0:n0 + 512])


def _proj_call(x, mod6, w_in_bf, S):
    N = x.shape[0]
    tm = 256
    tps = S // tm
    return pl.pallas_call(
        _proj_kernel,
        grid=(N // tm,),
        in_specs=[pl.BlockSpec((tm, D_MODEL), lambda i: (i, 0)),
                  _mod_spec(0, tps), _mod_spec(1, tps),
                  pl.BlockSpec((D_MODEL, IN_WIDTH), lambda i: (0, 0))],
        out_specs=pl.BlockSpec((tm, IN_WIDTH), lambda i: (i, 0)),
        out_shape=jax.ShapeDtypeStruct((N, IN_WIDTH), f32),
        compiler_params=_params(("arbitrary",)),
        name="proj",
    )(x, mod6, mod6, w_in_bf)


HG_CHUNK = 128
HG_SUB = 16
HG_STEP_ROWS = 512


def _hgrn_chunk(q, z, v, lb, st, tri):
    one_m_lb = 1.0 - lb
    g = jnp.log(lb + one_m_lb * jax.nn.sigmoid(z))
    k = one_m_lb * jax.nn.sigmoid(-z)
    g_hi, g_lo = _split_bf16(g)
    b = _dot(tri, g_hi) + _dot(tri, g_lo)
    b_end = b[HG_CHUNK - 1:HG_CHUNK, :]
    inter = _nt((q * jnp.exp(b)).astype(bf16), st.astype(bf16))
    k_bf = k.astype(bf16)
    v_bf = v.astype(bf16)
    row_c = lax.broadcasted_iota(i32, (HG_CHUNK, HG_D), 0)
    row_s = lax.broadcasted_iota(i32, (HG_SUB, HG_D), 0)
    blocks = []
    for sub in range(HG_CHUNK // HG_SUB):
        r0 = sub * HG_SUB
        bs = b[r0:r0 + HG_SUB]
        qs = q[r0:r0 + HG_SUB]
        ks = k[r0:r0 + HG_SUB]
        vs = v[r0:r0 + HG_SUB]
        rows = []
        for t in range(HG_SUB):
            m = row_s <= t
            e = jnp.where(m, jnp.exp(jnp.where(m, bs[t:t + 1] - bs, 0.0)), 0.0)
            p = (qs[t:t + 1] * ks) * e
            srow = jnp.sum(p, axis=-1, keepdims=True)
            rows.append(jnp.sum(srow * vs, axis=0, keepdims=True))
        o_sub = jnp.concatenate(rows, axis=0)
        if sub > 0:
            bref = b[r0 - 1:r0, :]
            qi = (qs * jnp.exp(bs - bref)).astype(bf16)
            past = row_c < r0
            ki = jnp.where(past, k * jnp.exp(jnp.where(past, bref - b, 0.0)), 0.0).astype(bf16)
            o_sub = o_sub + _dot(_nt(qi, ki).astype(bf16), v_bf)
        blocks.append(o_sub)
    intra = jnp.concatenate(blocks, axis=0)
    kd = (k * jnp.exp(b_end - b)).astype(bf16)
    st_new = st * jnp.exp(b_end) + _tn(v_bf, kd)
    del k_bf
    return inter + intra, st_new


def _hgrn_kernel(l_ref, q_ref, z_ref, v_ref, lbl_ref, o_ref, st_ref):
    @pl.when(pl.program_id(2) == 0)
    def _():
        st_ref[...] = jnp.zeros_like(st_ref)

    logits = lbl_ref[...]
    e = jnp.exp(logits - jnp.max(logits, axis=0, keepdims=True))
    p = e / jnp.sum(e, axis=0, keepdims=True)
    rid = lax.broadcasted_iota(i32, p.shape, 0)
    l = l_ref[0]
    lb = jnp.sum(jnp.where((rid >= 1) & (rid <= l), p, 0.0), axis=0, keepdims=True)

    r = lax.broadcasted_iota(i32, (HG_CHUNK, HG_CHUNK), 0)
    c = lax.broadcasted_iota(i32, (HG_CHUNK, HG_CHUNK), 1)
    tri = jnp.where(c <= r, 1.0, 0.0).astype(bf16)

    def body(ci, st):
        r0 = pl.multiple_of(ci * HG_CHUNK, HG_CHUNK)
        out, st = _hgrn_chunk(q_ref[pl.ds(r0, HG_CHUNK), :], z_ref[pl.ds(r0, HG_CHUNK), :],
                              v_ref[pl.ds(r0, HG_CHUNK), :], lb, st, tri)
        o_ref[pl.ds(r0, HG_CHUNK), :] = out
        return st

    st_ref[...] = lax.fori_loop(0, HG_STEP_ROWS // HG_CHUNK, body, st_ref[...])


def _hgrn_call(layer, proj, lb_logits, B, S):
    N = proj.shape[0]
    R = HG_STEP_ROWS
    spb = S // R

    def col(c0):
        return pl.BlockSpec((R, HG_D), lambda b, h, s, l: (b * spb + s, c0 + h))

    grid_spec = pltpu.PrefetchScalarGridSpec(
        num_scalar_prefetch=1,
        grid=(B, HG_HEADS, spb),
        in_specs=[col(COL_QA), col(COL_FA), col(COL_IA),
                  pl.BlockSpec((DEPTH, HG_D), lambda b, h, s, l: (0, h))],
        out_specs=pl.BlockSpec((R, HG_D), lambda b, h, s, l: (b * spb + s, h)),
        scratch_shapes=[pltpu.VMEM((HG_D, HG_D), f32)],
    )
    return pl.pallas_call(
        _hgrn_kernel,
        grid_spec=grid_spec,
        out_shape=jax.ShapeDtypeStruct((N, HG_W), f32),
        compiler_params=_params(("arbitrary", "arbitrary", "arbitrary")),
        name="hgrn",
    )(layer, proj, proj, proj, lb_logits)


SB_SCALE = 1.0 / math.sqrt(SB_DH)


def _sb_kernel(q_ref, k_ref, v_ref, o_ref):
    i = pl.program_id(2)
    T = SB_BLOCK
    q2 = q_ref[...] * SB_SCALE
    lane = lax.broadcasted_iota(i32, (T, T), 1)
    row = lax.broadcasted_iota(i32, (T, T), 0)
    lane2 = lax.broadcasted_iota(i32, (T, 2 * T), 1)
    row2 = lax.broadcasted_iota(i32, (T, 2 * T), 0)
    u = jnp.where((row2 > lane2) | (lane2 >= T), 1.0, 0.0).astype(bf16)
    causal = lane < row

    def block(qm, j, carry, acc, diag):
        r0 = pl.multiple_of(j * T, T)
        kj = k_ref[pl.ds(r0, T), :].astype(bf16)
        vj = v_ref[pl.ds(r0, T), :].astype(bf16)
        z = _nt(qm, kj)
        ls = jnp.minimum(z, 0.0) - jnp.log1p(jnp.exp(-jnp.abs(z)))
        l1m = ls - z
        if diag:
            l1m = jnp.where(causal, l1m, 0.0)
        hi, lo = _split_bf16(l1m)
        r = _dot(hi, u) + _dot(lo, u)
        w = jnp.exp(ls + r[:, :T] + carry)
        if diag:
            w = jnp.where(causal, w, 0.0)
        acc = acc + _dot(w.astype(bf16), vj)
        carry = carry + r[:, T:]
        return carry, acc

    outs = []
    for hh in range(2):
        in_head = (lane >= hh * SB_DH) & (lane < (hh + 1) * SB_DH)
        qm = jnp.where(in_head, q2, 0.0).astype(bf16)
        zero = jnp.zeros((T, T), f32)
        carry, acc = block(qm, i, zero, zero, True)

        def body(it, ca, qm=qm):
            return block(qm, i - 1 - it, ca[0], ca[1], False)

        carry, acc = lax.fori_loop(0, i, body, (carry, acc))
        outs.append(acc)
    o_ref[...] = jnp.where(lane < SB_DH, outs[0], outs[1])


def _sb_call(proj, B, S):
    N = proj.shape[0]
    nq = S // SB_BLOCK
    return pl.pallas_call(
        _sb_kernel,
        grid=(B, SB_HEADS // 2, nq),
        in_specs=[pl.BlockSpec((SB_BLOCK, LANES), lambda b, p, i: (b * nq + i, COL_QB + p)),
                  pl.BlockSpec((S, LANES), lambda b, p, i: (b, COL_KB + p)),
                  pl.BlockSpec((S, LANES), lambda b, p, i: (b, COL_VB + p))],
        out_specs=pl.BlockSpec((SB_BLOCK, LANES), lambda b, p, i: (b * nq + i, p)),
        out_shape=jax.ShapeDtypeStruct((N, SB_W), f32),
        compiler_params=_params(("arbitrary", "arbitrary", "arbitrary")),
        name="sb",
    )(proj, proj, proj)


def _layer_norm(r, g, b):
    mu = jnp.mean(r, axis=-1, keepdims=True)
    d = r - mu
    var = jnp.mean(d * d, axis=-1, keepdims=True)
    return d * lax.rsqrt(var + LN_EPS) * g + b


def _merge_kernel(x_ref, oa_ref, ga_ref, ob_ref, gta0_ref, gta1_ref, gtb0_ref, gtb1_ref,
                  gt1_ref, sh2_ref, sc2_ref, hgn_ref, wua_ref, wub_ref, wo_ref, wpq_ref,
                  lng_ref, lnb_ref, x1_ref, hf_ref, pq_ref):
    oa = oa_ref[...]
    hgn = hgn_ref[...]
    segs = []
    for h in range(HG_HEADS):
        seg = oa[:, h * HG_D:(h + 1) * HG_D]
        ms = jnp.mean(seg * seg, axis=-1, keepdims=True)
        segs.append(seg * lax.rsqrt(ms + RMS_EPS) * hgn[:, h * HG_D:(h + 1) * HG_D])
    ga = ga_ref[...]
    oa_n = jnp.concatenate(segs, axis=-1) * (ga * jax.nn.sigmoid(ga))
    ma = _dot(oa_n.astype(bf16), wua_ref[...])
    mb = _dot(ob_ref[...].astype(bf16), wub_ref[...])
    gate_a = jnp.concatenate([gta0_ref[...], gta1_ref[...]], axis=-1)
    gate_b = jnp.concatenate([gtb0_ref[...], gtb1_ref[...]], axis=-1)
    merged = jax.nn.sigmoid(gate_a) * ma + jax.nn.sigmoid(gate_b) * mb
    y = _dot(merged.astype(bf16), wo_ref[...]) * gt1_ref[...]
    x1 = _layer_norm(DN_ALPHA * x_ref[...] + y, lng_ref[...], lnb_ref[...])
    x1_ref[...] = x1
    hf = x1 * (1.0 + sc2_ref[...]) + sh2_ref[...]
    hf_ref[...] = hf
    pq_ref[...] = _dot(hf.astype(bf16), wpq_ref[...])


def _merge_call(x, oa, proj, ob, mod6, hgn, wua, wub, wo, wpq, lng, lnb, S):
    N = x.shape[0]
    tm = 256
    tps = S // tm

    def full(shape):
        return pl.BlockSpec(shape, lambda i: (0,) * len(shape))

    def p512(c):
        return pl.BlockSpec((tm, 512), lambda i: (i, c))

    row = pl.BlockSpec((tm, D_MODEL), lambda i: (i, 0))
    out = jax.ShapeDtypeStruct((N, D_MODEL), f32)
    return pl.pallas_call(
        _merge_kernel,
        grid=(N // tm,),
        in_specs=[row, pl.BlockSpec((tm, HG_W), lambda i: (i, 0)), p512(COL512_GA),
                  pl.BlockSpec((tm, SB_W), lambda i: (i, 0)),
                  p512(COL512_GATE_A), p512(COL512_GATE_A + 1),
                  p512(COL512_GATE_B), p512(COL512_GATE_B + 1),
                  _mod_spec(2, tps), _mod_spec(3, tps), _mod_spec(4, tps),
                  full((1, HG_W)), full((HG_W, D_MODEL)), full((SB_W, D_MODEL)),
                  full((D_MODEL, D_MODEL)), full((D_MODEL, D_MODEL)),
                  full((1, D_MODEL)), full((1, D_MODEL))],
        out_specs=[row, row, row],
        out_shape=[out, out, out],
        compiler_params=_params(("arbitrary",)),
        name="merge",
    )(x, oa, proj, ob, proj, proj, proj, proj, mod6, mod6, mod6, hgn, wua, wub, wo, wpq, lng, lnb)


TOPK_TOKENS = 128


def _top16(s, payload=None):
    R = s.shape[0]
    rid = lax.broadcasted_iota(i32, s.shape, 0)
    vals, ids = [], []
    for _ in range(PEER_TOPK):
        m = jnp.max(s, axis=0, keepdims=True)
        first = jnp.min(jnp.where(s == m, rid, R), axis=0, keepdims=True)
        sel = rid == first
        vals.append(m)
        if payload is None:
            ids.append(first)
        else:
            ids.append(jnp.max(jnp.where(sel, payload, -1), axis=0, keepdims=True))
        s = jnp.where(sel, -jnp.inf, s)
    return vals, ids


def _topk_kernel(pq_ref, k1_ref, k2_ref, idx_ref, gate_ref, idx_t, gate_t):
    def head(h, _):
        c0 = pl.multiple_of(h * PEER_DKEY, PEER_DKEY)
        qh = pq_ref[:, pl.ds(c0, PEER_DKEY)].astype(bf16)
        s1 = _nt(k1_ref[h], qh)
        s2 = _nt(k2_ref[h], qh)
        v1, i1 = _top16(s1)
        v2, i2 = _top16(s2)
        v2a = jnp.concatenate(v2, axis=0)
        i2a = jnp.concatenate(i2, axis=0)
        cand = jnp.concatenate([v1[a] + v2a for a in range(PEER_TOPK)], axis=0)
        cidx = jnp.concatenate([i1[a] * PEER_NKEYS + i2a for a in range(PEER_TOPK)], axis=0)
        tv, ti = _top16(cand, cidx)
        tva = jnp.concatenate(tv, axis=0)
        e = jnp.exp(tva - tv[0])
        r0 = pl.multiple_of(h * PEER_TOPK, PEER_TOPK)
        gate_t[pl.ds(r0, PEER_TOPK), :] = e / jnp.sum(e, axis=0, keepdims=True)
        idx_t[pl.ds(r0, PEER_TOPK), :] = jnp.concatenate(ti, axis=0)
        return 0

    lax.fori_loop(0, PEER_HEADS, head, 0)
    idx_ref[...] = idx_t[...].T
    gate_ref[...] = gate_t[...].T


def _topk_call(pq, k1p, k2p):
    N = pq.shape[0]
    T = TOPK_TOKENS
    keys = pl.BlockSpec((PEER_HEADS, PEER_NKEYS, PEER_DKEY), lambda i: (0, 0, 0))
    return pl.pallas_call(
        _topk_kernel,
        grid=(N // T,),
        in_specs=[pl.BlockSpec((T, D_MODEL), lambda i: (i, 0)), keys, keys],
        out_specs=[pl.BlockSpec((T, PEER_PAIRS), lambda i: (i, 0)),
                   pl.BlockSpec((T, PEER_PAIRS), lambda i: (i, 0))],
        out_shape=[jax.ShapeDtypeStruct((N, PEER_PAIRS), i32),
                   jax.ShapeDtypeStruct((N, PEER_PAIRS), f32)],
        scratch_shapes=[pltpu.VMEM((PEER_PAIRS, T), i32), pltpu.VMEM((PEER_PAIRS, T), f32)],
        compiler_params=_params(("arbitrary",)),
        name="topk",
    )(pq, k1p, k2p)


PEER_TOKENS = 64
HI_MASK = -65536
INV_SQRT2 = 1.0 / math.sqrt(2.0)


def _unpack_row(tab_ref, e):
    w = tab_ref[pl.ds(pl.multiple_of(e * ROW_SUB, ROW_SUB), ROW_SUB), :]
    lo = lax.bitcast_convert_type(lax.shift_left(w, 16), f32)
    hi = lax.bitcast_convert_type(w & HI_MASK, f32)
    return lo, hi


def _peer_u_kernel(idx_ref, hf_ref, g_ref, tab_ref, o_ref, scr):
    T = o_ref.shape[0]
    ones = jnp.ones((SUBLANES, LANES), bf16)

    def tok(j, _):
        r0 = pl.multiple_of(j * SUBLANES, SUBLANES)
        hlo = hf_ref[pl.ds(r0, ROW_SUB), :]
        hhi = hf_ref[pl.ds(r0 + ROW_SUB, ROW_SUB), :]
        for k in range(PEER_PAIRS):
            lo, hi = _unpack_row(tab_ref, idx_ref[j, k])
            scr[k * SUBLANES:k * SUBLANES + ROW_SUB, :] = lo * hlo + hi * hhi
        a = scr[pl.ds(0, PEER_PAIRS, stride=SUBLANES), :]
        for s in range(1, ROW_SUB):
            a = a + scr[pl.ds(s, PEER_PAIRS, stride=SUBLANES), :]
        a_hi, a_lo = _split_bf16(a)
        dots = (_nt(ones, a_hi) + _nt(ones, a_lo))[0:1]
        act = 0.5 * dots * (1.0 + lax.erf(dots * INV_SQRT2))
        o_ref[pl.ds(j, 1), :] = act * g_ref[pl.ds(j, 1), :]
        return 0

    lax.fori_loop(0, T, tok, 0)


def _peer_v_kernel(idx_ref, w_ref, tab_ref, o_ref):
    T = o_ref.shape[0] // SUBLANES
    n_acc = 4

    def tok(j, _):
        acc_lo = [jnp.zeros((ROW_SUB, LANES), f32) for _ in range(n_acc)]
        acc_hi = [jnp.zeros((ROW_SUB, LANES), f32) for _ in range(n_acc)]
        for k in range(PEER_PAIRS):
            lo, hi = _unpack_row(tab_ref, idx_ref[j, k])
            wk = w_ref[j, k]
            acc_lo[k % n_acc] = acc_lo[k % n_acc] + wk * lo
            acc_hi[k % n_acc] = acc_hi[k % n_acc] + wk * hi
        lo = (acc_lo[0] + acc_lo[1]) + (acc_lo[2] + acc_lo[3])
        hi = (acc_hi[0] + acc_hi[1]) + (acc_hi[2] + acc_hi[3])
        r0 = pl.multiple_of(j * SUBLANES, SUBLANES)
        o_ref[pl.ds(r0, SUBLANES), :] = jnp.concatenate([lo, hi], axis=0)
        return 0

    lax.fori_loop(0, T, tok, 0)


def _table_spec():
    return pl.BlockSpec((PEER_N * ROW_SUB, LANES), lambda i: (0, 0), pipeline_mode=pl.Buffered(1))


def _smem_rows(T):
    return pl.BlockSpec((T, PEER_PAIRS), lambda i: (i, 0), memory_space=pltpu.SMEM)


def _peer_u_call(idx, hf8, gates, u_pk):
    N = idx.shape[0]
    T = PEER_TOKENS
    return pl.pallas_call(
        _peer_u_kernel,
        grid=(N // T,),
        in_specs=[_smem_rows(T),
                  pl.BlockSpec((T * SUBLANES, LANES), lambda i: (i, 0)),
                  pl.BlockSpec((T, PEER_PAIRS), lambda i: (i, 0)),
                  _table_spec()],
        out_specs=pl.BlockSpec((T, PEER_PAIRS), lambda i: (i, 0)),
        out_shape=jax.ShapeDtypeStruct((N, PEER_PAIRS), f32),
        scratch_shapes=[pltpu.VMEM((PEER_PAIRS * SUBLANES, LANES), f32)],
        compiler_params=_params(("arbitrary",)),
        name="peer_u",
    )(idx, hf8, gates, u_pk)


def _peer_v_call(idx, wgt, v_pk):
    N = idx.shape[0]
    T = PEER_TOKENS
    return pl.pallas_call(
        _peer_v_kernel,
        grid=(N // T,),
        in_specs=[_smem_rows(T), _smem_rows(T), _table_spec()],
        out_specs=pl.BlockSpec((T * SUBLANES, LANES), lambda i: (i, 0)),
        out_shape=jax.ShapeDtypeStruct((N * SUBLANES, LANES), f32),
        compiler_params=_params(("arbitrary",)),
        name="peer_v",
    )(idx, wgt, v_pk)


def _ln_kernel(x_ref, y_ref, gt_ref, g_ref, b_ref, o_ref):
    o_ref[...] = _layer_norm(DN_ALPHA * x_ref[...] + y_ref[...] * gt_ref[...], g_ref[...], b_ref[...])


def _ln_call(x1, y, mod6, lng, lnb, S):
    N = x1.shape[0]
    tm = 512
    row = pl.BlockSpec((tm, D_MODEL), lambda i: (i, 0))
    vec = pl.BlockSpec((1, D_MODEL), lambda i: (0, 0))
    return pl.pallas_call(
        _ln_kernel,
        grid=(N // tm,),
        in_specs=[row, row, _mod_spec(5, S // tm), vec, vec],
        out_specs=row,
        out_shape=jax.ShapeDtypeStruct((N, D_MODEL), f32),
        compiler_params=_params(("arbitrary",)),
        name="ln",
    )(x1, y, mod6, lng, lnb)


def _pack_table(t):
    tb = lax.bitcast_convert_type(t.astype(bf16), jnp.uint16).astype(jnp.uint32)
    word = tb[:, :ROW_WORDS] | (tb[:, ROW_WORDS:] << 16)
    return lax.bitcast_convert_type(word, i32).reshape(PEER_N * ROW_SUB, LANES)


def _pad_keys(sub_keys_l):
    half = PEER_DKEY // 2
    z = jnp.zeros((PEER_HEADS, PEER_NKEYS, half), f32)
    k1 = jnp.concatenate([sub_keys_l[0], z], axis=-1).astype(bf16)
    k2 = jnp.concatenate([z, sub_keys_l[1]], axis=-1).astype(bf16)
    return k1, k2


def _layer(x, c, lb_logits, B, S, layer, p):
    (w_ada, b_ada, w_in, hgn, wua, wub, wo, wpq, sub_keys, pu, pv, ln_g, ln_b) = p
    N = x.shape[0]
    mod6 = _mod_call(c, w_ada, b_ada.reshape(1, -1)).reshape(B, 6, 1, D_MODEL)
    proj = _proj_call(x, mod6, w_in.astype(bf16), S)
    oa = _hgrn_call(layer.reshape(1), proj, lb_logits, B, S)
    ob = _sb_call(proj, B, S)
    x1, hf, pq = _merge_call(x, oa, proj, ob, mod6, hgn.reshape(1, -1), wua.astype(bf16),
                             wub.astype(bf16), wo.astype(bf16), wpq.astype(bf16),
                             ln_g[0:1], ln_b[0:1], S)
    k1p, k2p = _pad_keys(sub_keys)
    idx, gates = _topk_call(pq, k1p, k2p)
    wgt = _peer_u_call(idx, hf.reshape(N * SUBLANES, LANES), gates, _pack_table(pu))
    y = _peer_v_call(idx, wgt, _pack_table(pv)).reshape(N, D_MODEL)
    return _ln_call(x1, y, mod6, ln_g[1:2], ln_b[1:2], S)


def kernel(x, c, w_ada, b_ada, w_in, lb_logits, hg_norm_g, w_up_a, w_up_b, w_o, w_pq, sub_keys,
           peer_u, peer_v, ln_g, ln_b):
    B, S, _ = x.shape
    xs = (jnp.arange(DEPTH, dtype=i32), w_ada, b_ada, w_in, hg_norm_g, w_up_a, w_up_b, w_o, w_pq,
          sub_keys, peer_u, peer_v, ln_g, ln_b)

    def step(xc, per_layer):
        return _layer(xc, c, lb_logits, B, S, per_layer[0], per_layer[1:]), None

    out, _ = lax.scan(step, x.reshape(B * S, D_MODEL), xs)
    return out.reshape(B, S, D_MODEL)
```

```python
import functools
import math

import jax
import jax.numpy as jnp
from jax import lax
from jax.experimental import pallas as pl
from jax.experimental.pallas import tpu as pltpu

f32 = jnp.float32
bf16 = jnp.bfloat16
i32 = jnp.int32

D_MODEL = 1024
DEPTH = 4
HG_HEADS = 4
HG_D = 128
HG_W = HG_HEADS * HG_D
SB_HEADS = 8
SB_DH = 64
SB_W = SB_HEADS * SB_DH
SB_BLOCK = 128
PEER_HEADS = 8
PEER_NKEYS = 128
PEER_N = PEER_NKEYS * PEER_NKEYS
PEER_DKEY = 128
PEER_TOPK = 16
PEER_PAIRS = PEER_HEADS * PEER_TOPK
IN_WIDTH = 4 * HG_W + 3 * SB_W + 2 * D_MODEL
DN_ALPHA = (2.0 * DEPTH) ** 0.25
LN_EPS = 1e-5
RMS_EPS = 1e-6

LANES = 128
SUBLANES = 8
ROW_WORDS = D_MODEL // 2
ROW_SUB = ROW_WORDS // LANES
VMEM_LIMIT = 48 * 1024 * 1024

COL_QA, COL_FA, COL_IA = 0, 4, 8
COL_QB, COL_KB, COL_VB = 16, 20, 24
COL512_GA, COL512_GATE_A, COL512_GATE_B = 3, 7, 9


def _nt(a, b):
    return lax.dot_general(a, b, (((1,), (1,)), ((), ())), preferred_element_type=f32)


def _tn(a, b):
    return lax.dot_general(a, b, (((0,), (0,)), ((), ())), preferred_element_type=f32)


def _dot(a, b):
    return jnp.dot(a, b, preferred_element_type=f32)


def _split_bf16(a):
    hi = a.astype(bf16)
    lo = (a - hi.astype(f32)).astype(bf16)
    return hi, lo


def _params(sem):
    return pltpu.CompilerParams(dimension_semantics=sem, vmem_limit_bytes=VMEM_LIMIT)


def _mod_kernel(c_ref, w_ref, b_ref, o_ref):
    c = c_ref[...]
    cond = (c * jax.nn.sigmoid(c)).astype(bf16)
    o_ref[...] = _dot(cond, w_ref[...].astype(bf16)) + b_ref[...]


def _mod_call(c, w_ada_l, b_ada_l):
    B = c.shape[0]
    tn = 1536
    return pl.pallas_call(
        _mod_kernel,
        grid=(6 * D_MODEL // tn,),
        in_specs=[pl.BlockSpec((B, D_MODEL), lambda j: (0, 0)),
                  pl.BlockSpec((D_MODEL, tn), lambda j: (0, j)),
                  pl.BlockSpec((1, tn), lambda j: (0, j))],
        out_specs=pl.BlockSpec((B, tn), lambda j: (0, j)),
        out_shape=jax.ShapeDtypeStruct((B, 6 * D_MODEL), f32),
        compiler_params=_params(("arbitrary",)),
        name="mod",
    )(c, w_ada_l, b_ada_l)


def _mod_spec(which, tiles_per_seq):
    return pl.BlockSpec((None, None, 1, D_MODEL), lambda i: (i // tiles_per_seq, which, 0, 0))


def _proj_kernel(x_ref, sh_ref, sc_ref, w_ref, o_ref):
    hm = (x_ref[...] * (1.0 + sc_ref[...]) + sh_ref[...]).astype(bf16)
    for n0 in range(0, IN_WIDTH, 512):
        o_ref[:, n0:n0 + 512] = _dot(hm, w_ref[:, n0:n0 + 512])


def _proj_call(x, mod6, w_in_bf, S):
    N = x.shape[0]
    tm = 256
    tps = S // tm
    return pl.pallas_call(
        _proj_kernel,
        grid=(N // tm,),
        in_specs=[pl.BlockSpec((tm, D_MODEL), lambda i: (i, 0)),
                  _mod_spec(0, tps), _mod_spec(1, tps),
                  pl.BlockSpec((D_MODEL, IN_WIDTH), lambda i: (0, 0))],
        out_specs=pl.BlockSpec((tm, IN_WIDTH), lambda i: (i, 0)),
        out_shape=jax.ShapeDtypeStruct((N, IN_WIDTH), f32),
        compiler_params=_params(("arbitrary",)),
        name="proj",
    )(x, mod6, mod6, w_in_bf)


HG_CHUNK = 128
HG_SUB = 16
HG_STEP_ROWS = 512


def _hgrn_chunk(q, z, v, lb, st, tri):
    one_m_lb = 1.0 - lb
    g = jnp.log(lb + one_m_lb * jax.nn.sigmoid(z))
    k = one_m_lb * jax.nn.sigmoid(-z)
    g_hi, g_lo = _split_bf16(g)
    b = _dot(tri, g_hi) + _dot(tri, g_lo)
    b_end = b[HG_CHUNK - 1:HG_CHUNK, :]
    inter = _nt((q * jnp.exp(b)).astype(bf16), st.astype(bf16))
    v_bf = v.astype(bf16)
    row_c = lax.broadcasted_iota(i32, (HG_CHUNK, HG_D), 0)
    row_s = lax.broadcasted_iota(i32, (HG_SUB, HG_D), 0)
    blocks = []
    for sub in range(HG_CHUNK // HG_SUB):
        r0 = sub * HG_SUB
        bs = b[r0:r0 + HG_SUB]
        qs = q[r0:r0 + HG_SUB]
        ks = k[r0:r0 + HG_SUB]
        vs = v[r0:r0 + HG_SUB]
        rows = []
        for t in range(HG_SUB):
            m = row_s <= t
            e = jnp.where(m, jnp.exp(jnp.where(m, bs[t:t + 1] - bs, 0.0)), 0.0)
            p = (qs[t:t + 1] * ks) * e
            srow = jnp.sum(p, axis=-1, keepdims=True)
            rows.append(jnp.sum(srow * vs, axis=0, keepdims=True))
        o_sub = jnp.concatenate(rows, axis=0)
        if sub > 0:
            bref = b[r0 - 1:r0, :]
            qi = (qs * jnp.exp(bs - bref)).astype(bf16)
            past = row_c < r0
            ki = jnp.where(past, k * jnp.exp(jnp.where(past, bref - b, 0.0)), 0.0).astype(bf16)
            o_sub = o_sub + _dot(_nt(qi, ki).astype(bf16), v_bf)
        blocks.append(o_sub)
    intra = jnp.concatenate(blocks, axis=0)
    kd = (k * jnp.exp(b_end - b)).astype(bf16)
    st_new = st * jnp.exp(b_end) + _tn(v_bf, kd)
    return inter + intra, st_new


def _hgrn_kernel(l_ref, q_ref, z_ref, v_ref, lbl_ref, o_ref, st_ref):
    @pl.when(pl.program_id(2) == 0)
    def _():
        st_ref[...] = jnp.zeros_like(st_ref)

    logits = lbl_ref[...]
    e = jnp.exp(logits - jnp.max(logits, axis=0, keepdims=True))
    p = e / jnp.sum(e, axis=0, keepdims=True)
    rid = lax.broadcasted_iota(i32, p.shape, 0)
    l = l_ref[0]
    lb = jnp.sum(jnp.where((rid >= 1) & (rid <= l), p, 0.0), axis=0, keepdims=True)

    r = lax.broadcasted_iota(i32, (HG_CHUNK, HG_CHUNK), 0)
    c = lax.broadcasted_iota(i32, (HG_CHUNK, HG_CHUNK), 1)
    tri = jnp.where(c <= r, 1.0, 0.0).astype(bf16)

    def body(ci, st):
        r0 = pl.multiple_of(ci * HG_CHUNK, HG_CHUNK)
        out, st = _hgrn_chunk(q_ref[pl.ds(r0, HG_CHUNK), :], z_ref[pl.ds(r0, HG_CHUNK), :],
                              v_ref[pl.ds(r0, HG_CHUNK), :], lb, st, tri)
        o_ref[pl.ds(r0, HG_CHUNK), :] = out
        return st

    st_ref[...] = lax.fori_loop(0, HG_STEP_ROWS // HG_CHUNK, body, st_ref[...])


def _hgrn_call(layer, proj, lb_logits, B, S):
    N = proj.shape[0]
    R = HG_STEP_ROWS
    spb = S // R

    def col(c0):
        return pl.BlockSpec((R, HG_D), lambda b, h, s, l: (b * spb + s, c0 + h))

    grid_spec = pltpu.PrefetchScalarGridSpec(
        num_scalar_prefetch=1,
        grid=(B, HG_HEADS, spb),
        in_specs=[col(COL_QA), col(COL_FA), col(COL_IA),
                  pl.BlockSpec((DEPTH, HG_D), lambda b, h, s, l: (0, h))],
        out_specs=pl.BlockSpec((R, HG_D), lambda b, h, s, l: (b * spb + s, h)),
        scratch_shapes=[pltpu.VMEM((HG_D, HG_D), f32)],
    )
    return pl.pallas_call(
        _hgrn_kernel,
        grid_spec=grid_spec,
        out_shape=jax.ShapeDtypeStruct((N, HG_W), f32),
        compiler_params=_params(("arbitrary", "arbitrary", "arbitrary")),
        name="hgrn",
    )(layer, proj, proj, proj, lb_logits)


SB_SCALE = 1.0 / math.sqrt(SB_DH)
SB_GROUP = 3
SB_DEAD = -104.0
SB_NEVER = -(1 << 20)


def _sb_kernel(q_ref, k_ref, v_ref, o_ref):
    i = pl.program_id(2)
    T = SB_BLOCK
    q2 = q_ref[...] * SB_SCALE
    lane = lax.broadcasted_iota(i32, (T, T), 1)
    lane2 = lax.broadcasted_iota(i32, (T, 2 * T), 1)
    row2 = lax.broadcasted_iota(i32, (T, 2 * T), 0)
    u = jnp.where((row2 > lane2) | (lane2 >= T), 1.0, 0.0).astype(bf16)
    qm = jnp.concatenate([jnp.where(lane < SB_DH, q2, 0.0), jnp.where(lane >= SB_DH, q2, 0.0)],
                         axis=0).astype(bf16)
    lane_s = lax.broadcasted_iota(i32, (2 * T, T), 1)
    row_s = lax.broadcasted_iota(i32, (2 * T, T), 0)
    key_minus_query = lane_s - (row_s & (T - 1))

    def trip(state):
        j_hi, carry, acc = state
        ks, vs, allowed = [], [], []
        for b in range(SB_GROUP):
            j = j_hi - b
            r0 = pl.multiple_of(jnp.maximum(j, 0) * T, T)
            ks.append(k_ref[pl.ds(r0, T), :].astype(bf16))
            vs.append(v_ref[pl.ds(r0, T), :].astype(bf16))
            allowed.append(key_minus_query < jnp.where(j >= 0, (i - j) * T, SB_NEVER))
        zs = [_nt(qm, kj) for kj in ks]
        lss, his, los = [], [], []
        for z, ok in zip(zs, allowed):
            ls = jnp.minimum(z, 0.0) - jnp.log(1.0 + jnp.exp(-jnp.abs(z)))
            l1m = jnp.where(ok, ls - z, 0.0)
            hi, lo = _split_bf16(l1m)
            lss.append(ls)
            his.append(hi)
            los.append(lo)
        rs = [_dot(hi, u) + _dot(lo, u) for hi, lo in zip(his, los)]
        ws = []
        for ls, r, ok in zip(lss, rs, allowed):
            ws.append(jnp.where(ok, jnp.exp(ls + r[:, :T] + carry), 0.0).astype(bf16))
            carry = carry + r[:, T:]
        for w, vj in zip(ws, vs):
            acc = acc + _dot(w, vj)
        return j_hi - SB_GROUP, carry, acc

    def live(state):
        j_hi, carry, _ = state
        return (j_hi >= 0) & (jnp.max(carry) > SB_DEAD)

    zero = jnp.zeros((2 * T, T), f32)
    _, _, acc = lax.while_loop(live, trip, (i, zero, zero))
    o_ref[...] = jnp.where(lane < SB_DH, acc[:T], acc[T:])


def _sb_call(proj, B, S):
    N = proj.shape[0]
    nq = S // SB_BLOCK
    return pl.pallas_call(
        _sb_kernel,
        grid=(B, SB_HEADS // 2, nq),
        in_specs=[pl.BlockSpec((SB_BLOCK, LANES), lambda b, p, i: (b * nq + i, COL_QB + p)),
                  pl.BlockSpec((S, LANES), lambda b, p, i: (b, COL_KB + p)),
                  pl.BlockSpec((S, LANES), lambda b, p, i: (b, COL_VB + p))],
        out_specs=pl.BlockSpec((SB_BLOCK, LANES), lambda b, p, i: (b * nq + i, p)),
        out_shape=jax.ShapeDtypeStruct((N, SB_W), f32),
        compiler_params=_params(("arbitrary", "arbitrary", "arbitrary")),
        name="sb",
    )(proj, proj, proj)


def _layer_norm(r, g, b):
    mu = jnp.mean(r, axis=-1, keepdims=True)
    d = r - mu
    var = jnp.mean(d * d, axis=-1, keepdims=True)
    return d * lax.rsqrt(var + LN_EPS) * g + b


def _merge_kernel(x_ref, oa_ref, ga_ref, ob_ref, gta0_ref, gta1_ref, gtb0_ref, gtb1_ref,
                  gt1_ref, sh2_ref, sc2_ref, hgn_ref, wua_ref, wub_ref, wo_ref, wpq_ref,
                  lng_ref, lnb_ref, x1_ref, hf_ref, pq_ref):
    oa = oa_ref[...]
    hgn = hgn_ref[...]
    segs = []
    for h in range(HG_HEADS):
        seg = oa[:, h * HG_D:(h + 1) * HG_D]
        ms = jnp.mean(seg * seg, axis=-1, keepdims=True)
        segs.append(seg * lax.rsqrt(ms + RMS_EPS) * hgn[:, h * HG_D:(h + 1) * HG_D])
    ga = ga_ref[...]
    oa_n = jnp.concatenate(segs, axis=-1) * (ga * jax.nn.sigmoid(ga))
    ma = _dot(oa_n.astype(bf16), wua_ref[...])
    mb = _dot(ob_ref[...].astype(bf16), wub_ref[...])
    gate_a = jnp.concatenate([gta0_ref[...], gta1_ref[...]], axis=-1)
    gate_b = jnp.concatenate([gtb0_ref[...], gtb1_ref[...]], axis=-1)
    merged = jax.nn.sigmoid(gate_a) * ma + jax.nn.sigmoid(gate_b) * mb
    y = _dot(merged.astype(bf16), wo_ref[...]) * gt1_ref[...]
    x1 = _layer_norm(DN_ALPHA * x_ref[...] + y, lng_ref[...], lnb_ref[...])
    x1_ref[...] = x1
    hf = x1 * (1.0 + sc2_ref[...]) + sh2_ref[...]
    hf_ref[...] = hf
    pq_ref[...] = _dot(hf.astype(bf16), wpq_ref[...])


def _merge_call(x, oa, proj, ob, mod6, hgn, wua, wub, wo, wpq, lng, lnb, S):
    N = x.shape[0]
    tm = 256
    tps = S // tm

    def full(shape):
        return pl.BlockSpec(shape, lambda i: (0,) * len(shape))

    def p512(c):
        return pl.BlockSpec((tm, 512), lambda i: (i, c))

    row = pl.BlockSpec((tm, D_MODEL), lambda i: (i, 0))
    out = jax.ShapeDtypeStruct((N, D_MODEL), f32)
    return pl.pallas_call(
        _merge_kernel,
        grid=(N // tm,),
        in_specs=[row, pl.BlockSpec((tm, HG_W), lambda i: (i, 0)), p512(COL512_GA),
                  pl.BlockSpec((tm, SB_W), lambda i: (i, 0)),
                  p512(COL512_GATE_A), p512(COL512_GATE_A + 1),
                  p512(COL512_GATE_B), p512(COL512_GATE_B + 1),
                  _mod_spec(2, tps), _mod_spec(3, tps), _mod_spec(4, tps),
                  full((1, HG_W)), full((HG_W, D_MODEL)), full((SB_W, D_MODEL)),
                  full((D_MODEL, D_MODEL)), full((D_MODEL, D_MODEL)),
                  full((1, D_MODEL)), full((1, D_MODEL))],
        out_specs=[row, row, row],
        out_shape=[out, out, out],
        compiler_params=_params(("arbitrary",)),
        name="merge",
    )(x, oa, proj, ob, proj, proj, proj, proj, mod6, mod6, mod6, hgn, wua, wub, wo, wpq, lng, lnb)


TOPK_TOKENS = 256


def _top16(s, payload=None):
    R = s.shape[0]
    rid = lax.broadcasted_iota(i32, s.shape, 0).astype(f32)
    vals, ids = [], []
    for _ in range(PEER_TOPK):
        m = jnp.max(s, axis=0, keepdims=True)
        first = jnp.min(jnp.where(s == m, rid, float(R)), axis=0, keepdims=True)
        sel = rid == first
        vals.append(m)
        if payload is None:
            ids.append(first.astype(i32))
        else:
            ids.append(jnp.max(jnp.where(sel, payload, -1), axis=0, keepdims=True))
        s = jnp.where(sel, -jnp.inf, s)
    return vals, ids


def _topk_kernel(pq_ref, k1_ref, k2_ref, idx_ref, gate_ref, idx_t, gate_t):
    H = SUBLANES

    def head(h, _):
        c0 = pl.multiple_of(h * PEER_DKEY, PEER_DKEY)
        qh = pq_ref[:, pl.ds(c0, PEER_DKEY)].astype(bf16)
        s1 = _nt(k1_ref[h], qh)
        s2 = _nt(k2_ref[h], qh)
        v1, i1 = _top16(s1)
        v2, i2 = _top16(s2)
        v1a, i1a = jnp.concatenate(v1, axis=0), jnp.concatenate(i1, axis=0)
        v2a, i2a = jnp.concatenate(v2, axis=0), jnp.concatenate(i2, axis=0)
        cand = ([v1[0] + v2a[:H], v1[0] + v2a[H:]] + [v1[a] + v2a[:H] for a in range(1, H)]
                + [v1a[H:] + v2[0]])
        cidx = ([i1[0] * PEER_NKEYS + i2a[:H], i1[0] * PEER_NKEYS + i2a[H:]]
                + [i1[a] * PEER_NKEYS + i2a[:H] for a in range(1, H)] + [i1a[H:] * PEER_NKEYS + i2[0]])
        tv, ti = _top16(jnp.concatenate(cand, axis=0), jnp.concatenate(cidx, axis=0) * ROW_SUB)
        tva = jnp.concatenate(tv, axis=0)
        e = jnp.exp(tva - tv[0])
        r0 = pl.multiple_of(h * PEER_TOPK, PEER_TOPK)
        gate_t[pl.ds(r0, PEER_TOPK), :] = e / jnp.sum(e, axis=0, keepdims=True)
        idx_t[pl.ds(r0, PEER_TOPK), :] = jnp.concatenate(ti, axis=0)
        return 0

    lax.fori_loop(0, PEER_HEADS, head, 0)
    idx_ref[...] = idx_t[...].T
    gate_ref[...] = gate_t[...].T


def _topk_call(pq, k1p, k2p):
    N = pq.shape[0]
    T = TOPK_TOKENS
    keys = pl.BlockSpec((PEER_HEADS, PEER_NKEYS, PEER_DKEY), lambda i: (0, 0, 0))
    return pl.pallas_call(
        _topk_kernel,
        grid=(N // T,),
        in_specs=[pl.BlockSpec((T, D_MODEL), lambda i: (i, 0)), keys, keys],
        out_specs=[pl.BlockSpec((T, PEER_PAIRS), lambda i: (i, 0)),
                   pl.BlockSpec((T, PEER_PAIRS), lambda i: (i, 0))],
        out_shape=[jax.ShapeDtypeStruct((N, PEER_PAIRS), i32),
                   jax.ShapeDtypeStruct((N, PEER_PAIRS), f32)],
        scratch_shapes=[pltpu.VMEM((PEER_PAIRS, T), i32), pltpu.VMEM((PEER_PAIRS, T), f32)],
        compiler_params=_params(("arbitrary",)),
        name="topk",
    )(pq, k1p, k2p)


PEER_TOKENS = 64
HI_MASK = -65536
LO_MASK = 65535
INV_SQRT2 = 1.0 / math.sqrt(2.0)


def _unpack_row(tab_ref, e4):
    w = tab_ref[pl.ds(pl.multiple_of(e4, ROW_SUB), ROW_SUB), :]
    lo = lax.bitcast_convert_type(lax.shift_left(w, 16), f32)
    hi = lax.bitcast_convert_type(w & HI_MASK, f32)
    return lo, hi


def _peer_u_kernel(idx_s, idx_ref, hf_ref, g_ref, tab_ref, o_ref, scr, a_all):
    T = o_ref.shape[0]
    G = SUBLANES
    ones = jnp.ones((SUBLANES, LANES), bf16)
    sub = lax.broadcasted_iota(i32, (G, PEER_PAIRS), 0)

    def group(gi, _):
        def tok(t, _):
            j = gi * G + t
            r0 = pl.multiple_of(j * SUBLANES, SUBLANES)
            hlo = hf_ref[pl.ds(r0, ROW_SUB), :]
            hhi = hf_ref[pl.ds(r0 + ROW_SUB, ROW_SUB), :]
            k0 = j * PEER_PAIRS
            for k in range(PEER_PAIRS):
                lo, hi = _unpack_row(tab_ref, idx_s[k0 + k])
                scr[k * SUBLANES:k * SUBLANES + ROW_SUB, :] = lo * hlo + hi * hhi
            a = scr[pl.ds(0, PEER_PAIRS, stride=SUBLANES), :]
            for s in range(1, ROW_SUB):
                a = a + scr[pl.ds(s, PEER_PAIRS, stride=SUBLANES), :]
            a_all[pl.ds(pl.multiple_of(t * PEER_PAIRS, PEER_PAIRS), PEER_PAIRS), :] = a
            return 0

        lax.fori_loop(0, G, tok, 0)
        a_hi, a_lo = _split_bf16(a_all[...])
        r = _nt(ones, a_hi) + _nt(ones, a_lo)
        dots = r[:, 0:PEER_PAIRS]
        for t in range(1, G):
            dots = jnp.where(sub == t, r[:, t * PEER_PAIRS:(t + 1) * PEER_PAIRS], dots)
        g0 = pl.multiple_of(gi * G, G)
        act = 0.5 * dots * (1.0 + lax.erf(dots * INV_SQRT2))
        wgt = (act * g_ref[pl.ds(g0, G), :]).astype(bf16).astype(f32)
        o_ref[pl.ds(g0, G), :] = lax.bitcast_convert_type(wgt, i32) | idx_ref[pl.ds(g0, G), :]
        return 0

    lax.fori_loop(0, T // G, group, 0)


def _peer_v_kernel(pk_s, tab_ref, o_ref):
    T = o_ref.shape[0] // SUBLANES
    n_acc = 2

    def tok(j, _):
        zero = jnp.zeros((ROW_SUB, LANES), f32)
        acc_lo, acc_hi = [zero] * n_acc, [zero] * n_acc
        k0 = j * PEER_PAIRS
        for k in range(PEER_PAIRS):
            word = pk_s[k0 + k]
            lo, hi = _unpack_row(tab_ref, word & LO_MASK)
            wk = lax.bitcast_convert_type(jnp.full((ROW_SUB, LANES), word, i32) & HI_MASK, f32)
            acc_lo[k % n_acc] = acc_lo[k % n_acc] + wk * lo
            acc_hi[k % n_acc] = acc_hi[k % n_acc] + wk * hi
        lo = acc_lo[0] + acc_lo[1]
        hi = acc_hi[0] + acc_hi[1]
        r0 = pl.multiple_of(j * SUBLANES, SUBLANES)
        o_ref[pl.ds(r0, SUBLANES), :] = jnp.concatenate([lo, hi], axis=0)
        return 0

    lax.fori_loop(0, T, tok, 0)


def _table_spec():
    return pl.BlockSpec((PEER_N * ROW_SUB, LANES), lambda i: (0, 0), pipeline_mode=pl.Buffered(1))


def _smem_flat(T):
    return pl.BlockSpec((T * PEER_PAIRS,), lambda i: (i,), memory_space=pltpu.SMEM)


def _peer_u_call(idx4, hf8, gates, u_pk):
    N = idx4.shape[0]
    T = PEER_TOKENS
    rows = pl.BlockSpec((T, PEER_PAIRS), lambda i: (i, 0))
    return pl.pallas_call(
        _peer_u_kernel,
        grid=(N // T,),
        in_specs=[_smem_flat(T), rows,
                  pl.BlockSpec((T * SUBLANES, LANES), lambda i: (i, 0)),
                  rows, _table_spec()],
        out_specs=rows,
        out_shape=jax.ShapeDtypeStruct((N, PEER_PAIRS), i32),
        scratch_shapes=[pltpu.VMEM((PEER_PAIRS * SUBLANES, LANES), f32),
                        pltpu.VMEM((SUBLANES * PEER_PAIRS, LANES), f32)],
        compiler_params=_params(("arbitrary",)),
        name="peer_u",
    )(idx4.reshape(-1), idx4, hf8, gates, u_pk)


def _peer_v_call(packed, v_pk):
    N = packed.shape[0]
    T = PEER_TOKENS
    return pl.pallas_call(
        _peer_v_kernel,
        grid=(N // T,),
        in_specs=[_smem_flat(T), _table_spec()],
        out_specs=pl.BlockSpec((T * SUBLANES, LANES), lambda i: (i, 0)),
        out_shape=jax.ShapeDtypeStruct((N * SUBLANES, LANES), f32),
        compiler_params=_params(("arbitrary",)),
        name="peer_v",
    )(packed.reshape(-1), v_pk)


def _ln_kernel(x_ref, y_ref, gt_ref, g_ref, b_ref, o_ref):
    o_ref[...] = _layer_norm(DN_ALPHA * x_ref[...] + y_ref[...] * gt_ref[...], g_ref[...], b_ref[...])


def _ln_call(x1, y, mod6, lng, lnb, S):
    N = x1.shape[0]
    tm = 512
    row = pl.BlockSpec((tm, D_MODEL), lambda i: (i, 0))
    vec = pl.BlockSpec((1, D_MODEL), lambda i: (0, 0))
    return pl.pallas_call(
        _ln_kernel,
        grid=(N // tm,),
        in_specs=[row, row, _mod_spec(5, S // tm), vec, vec],
        out_specs=row,
        out_shape=jax.ShapeDtypeStruct((N, D_MODEL), f32),
        compiler_params=_params(("arbitrary",)),
        name="ln",
    )(x1, y, mod6, lng, lnb)


def _pack_table(t):
    tb = lax.bitcast_convert_type(t.astype(bf16), jnp.uint16).astype(jnp.uint32)
    word = tb[:, :ROW_WORDS] | (tb[:, ROW_WORDS:] << 16)
    return lax.bitcast_convert_type(word, i32).reshape(PEER_N * ROW_SUB, LANES)


def _pad_keys(sub_keys_l):
    half = PEER_DKEY // 2
    z = jnp.zeros((PEER_HEADS, PEER_NKEYS, half), f32)
    k1 = jnp.concatenate([sub_keys_l[0], z], axis=-1).astype(bf16)
    k2 = jnp.concatenate([z, sub_keys_l[1]], axis=-1).astype(bf16)
    return k1, k2


def _layer(x, c, lb_logits, B, S, layer, p):
    (w_ada, b_ada, w_in, hgn, wua, wub, wo, wpq, sub_keys, pu, pv, ln_g, ln_b) = p
    N = x.shape[0]
    mod6 = _mod_call(c, w_ada, b_ada.reshape(1, -1)).reshape(B, 6, 1, D_MODEL)
    proj = _proj_call(x, mod6, w_in.astype(bf16), S)
    oa = _hgrn_call(layer.reshape(1), proj, lb_logits, B, S)
    ob = _sb_call(proj, B, S)
    x1, hf, pq = _merge_call(x, oa, proj, ob, mod6, hgn.reshape(1, -1), wua.astype(bf16),
                             wub.astype(bf16), wo.astype(bf16), wpq.astype(bf16),
                             ln_g[0:1], ln_b[0:1], S)
    k1p, k2p = _pad_keys(sub_keys)
    idx4, gates = _topk_call(pq, k1p, k2p)
    packed = _peer_u_call(idx4, hf.reshape(N * SUBLANES, LANES), gates, _pack_table(pu))
    y = _peer_v_call(packed, _pack_table(pv)).reshape(N, D_MODEL)
    return _ln_call(x1, y, mod6, ln_g[1:2], ln_b[1:2], S)


def kernel(x, c, w_ada, b_ada, w_in, lb_logits, hg_norm_g, w_up_a, w_up_b, w_o, w_pq, sub_keys,
           peer_u, peer_v, ln_g, ln_b):
    B, S, _ = x.shape
    xs = (jnp.arange(DEPTH, dtype=i32), w_ada, b_ada, w_in, hg_norm_g, w_up_a, w_up_b, w_o, w_pq,
          sub_keys, peer_u, peer_v, ln_g, ln_b)

    def step(xc, per_layer):
        return _layer(xc, c, lb_logits, B, S, per_layer[0], per_layer[1:]), None

    out, _ = lax.scan(step, x.reshape(B * S, D_MODEL), xs)
    return out.reshape(B, S, D_MODEL)
```

```python
import functools
import math

import jax
import jax.numpy as jnp
from jax import lax
from jax.experimental import pallas as pl
from jax.experimental.pallas import tpu as pltpu

f32 = jnp.float32
bf16 = jnp.bfloat16
i32 = jnp.int32

D_MODEL = 1024
DEPTH = 4
HG_HEADS = 4
HG_D = 128
HG_W = HG_HEADS * HG_D
SB_HEADS = 8
SB_DH = 64
SB_W = SB_HEADS * SB_DH
SB_BLOCK = 128
PEER_HEADS = 8
PEER_NKEYS = 128
PEER_N = PEER_NKEYS * PEER_NKEYS
PEER_DKEY = 128
PEER_TOPK = 16
PEER_PAIRS = PEER_HEADS * PEER_TOPK
IN_WIDTH = 4 * HG_W + 3 * SB_W + 2 * D_MODEL
DN_ALPHA = (2.0 * DEPTH) ** 0.25
LN_EPS = 1e-5
RMS_EPS = 1e-6

LANES = 128
SUBLANES = 8
ROW_WORDS = D_MODEL // 2
ROW_SUB = ROW_WORDS // LANES
VMEM_LIMIT = 48 * 1024 * 1024

COL_QA, COL_FA, COL_IA = 0, 4, 8
COL_QB, COL_KB, COL_VB = 16, 20, 24
COL512_GA, COL512_GATE_A, COL512_GATE_B = 3, 7, 9


def _nt(a, b):
    return lax.dot_general(a, b, (((1,), (1,)), ((), ())), preferred_element_type=f32)


def _tn(a, b):
    return lax.dot_general(a, b, (((0,), (0,)), ((), ())), preferred_element_type=f32)


def _dot(a, b):
    return jnp.dot(a, b, preferred_element_type=f32)


def _split_bf16(a):
    hi = a.astype(bf16)
    lo = (a - hi.astype(f32)).astype(bf16)
    return hi, lo


def _params(sem):
    return pltpu.CompilerParams(dimension_semantics=sem, vmem_limit_bytes=VMEM_LIMIT)


def _mod_kernel(c_ref, w_ref, b_ref, o_ref):
    c = c_ref[...]
    cond = (c * jax.nn.sigmoid(c)).astype(bf16)
    o_ref[...] = _dot(cond, w_ref[...].astype(bf16)) + b_ref[...]


def _mod_call(c, w_ada_l, b_ada_l):
    B = c.shape[0]
    tn = 1536
    return pl.pallas_call(
        _mod_kernel,
        grid=(6 * D_MODEL // tn,),
        in_specs=[pl.BlockSpec((B, D_MODEL), lambda j: (0, 0)),
                  pl.BlockSpec((D_MODEL, tn), lambda j: (0, j)),
                  pl.BlockSpec((1, tn), lambda j: (0, j))],
        out_specs=pl.BlockSpec((B, tn), lambda j: (0, j)),
        out_shape=jax.ShapeDtypeStruct((B, 6 * D_MODEL), f32),
        compiler_params=_params(("arbitrary",)),
        name="mod",
    )(c, w_ada_l, b_ada_l)


def _mod_spec(which, tiles_per_seq):
    return pl.BlockSpec((None, None, 1, D_MODEL), lambda i: (i // tiles_per_seq, which, 0, 0))


def _proj_kernel(x_ref, sh_ref, sc_ref, w_ref, o_ref):
    hm = (x_ref[...] * (1.0 + sc_ref[...]) + sh_ref[...]).astype(bf16)
    for n0 in range(0, IN_WIDTH, 512):
        o_ref[:, n0:n0 + 512] = _dot(hm, w_ref[:, n0:n0 + 512])


def _proj_call(x, mod6, w_in_bf, S):
    N = x.shape[0]
    tm = 256
    tps = S // tm
    return pl.pallas_call(
        _proj_kernel,
        grid=(N // tm,),
        in_specs=[pl.BlockSpec((tm, D_MODEL), lambda i: (i, 0)),
                  _mod_spec(0, tps), _mod_spec(1, tps),
                  pl.BlockSpec((D_MODEL, IN_WIDTH), lambda i: (0, 0))],
        out_specs=pl.BlockSpec((tm, IN_WIDTH), lambda i: (i, 0)),
        out_shape=jax.ShapeDtypeStruct((N, IN_WIDTH), f32),
        compiler_params=_params(("arbitrary",)),
        name="proj",
    )(x, mod6, mod6, w_in_bf)


HG_CHUNK = 128
HG_SUB = 16
HG_STEP_ROWS = 512


def _hgrn_chunk(q, z, v, lb, st, tri):
    one_m_lb = 1.0 - lb
    g = jnp.log(lb + one_m_lb * jax.nn.sigmoid(z))
    k = one_m_lb * jax.nn.sigmoid(-z)
    g_hi, g_lo = _split_bf16(g)
    b = _dot(tri, g_hi) + _dot(tri, g_lo)
    b_end = b[HG_CHUNK - 1:HG_CHUNK, :]
    inter = _nt((q * jnp.exp(b)).astype(bf16), st.astype(bf16))
    v_bf = v.astype(bf16)
    row_c = lax.broadcasted_iota(i32, (HG_CHUNK, HG_D), 0)
    row_s = lax.broadcasted_iota(i32, (HG_SUB, HG_D), 0)
    blocks = []
    for sub in range(HG_CHUNK // HG_SUB):
        r0 = sub * HG_SUB
        bs = b[r0:r0 + HG_SUB]
        qs = q[r0:r0 + HG_SUB]
        ks = k[r0:r0 + HG_SUB]
        vs = v[r0:r0 + HG_SUB]
        rows = []
        for t in range(HG_SUB):
            m = row_s <= t
            e = jnp.where(m, jnp.exp(jnp.where(m, bs[t:t + 1] - bs, 0.0)), 0.0)
            p = (qs[t:t + 1] * ks) * e
            srow = jnp.sum(p, axis=-1, keepdims=True)
            rows.append(jnp.sum(srow * vs, axis=0, keepdims=True))
        o_sub = jnp.concatenate(rows, axis=0)
        if sub > 0:
            bref = b[r0 - 1:r0, :]
            qi = (qs * jnp.exp(bs - bref)).astype(bf16)
            past = row_c < r0
            ki = jnp.where(past, k * jnp.exp(jnp.where(past, bref - b, 0.0)), 0.0).astype(bf16)
            o_sub = o_sub + _dot(_nt(qi, ki).astype(bf16), v_bf)
        blocks.append(o_sub)
    intra = jnp.concatenate(blocks, axis=0)
    kd = (k * jnp.exp(b_end - b)).astype(bf16)
    st_new = st * jnp.exp(b_end) + _tn(v_bf, kd)
    return inter + intra, st_new


def _hgrn_kernel(l_ref, q_ref, z_ref, v_ref, lbl_ref, o_ref, st_ref):
    @pl.when(pl.program_id(2) == 0)
    def _():
        st_ref[...] = jnp.zeros_like(st_ref)

    logits = lbl_ref[...]
    e = jnp.exp(logits - jnp.max(logits, axis=0, keepdims=True))
    p = e / jnp.sum(e, axis=0, keepdims=True)
    rid = lax.broadcasted_iota(i32, p.shape, 0)
    l = l_ref[0]
    lb = jnp.sum(jnp.where((rid >= 1) & (rid <= l), p, 0.0), axis=0, keepdims=True)

    r = lax.broadcasted_iota(i32, (HG_CHUNK, HG_CHUNK), 0)
    c = lax.broadcasted_iota(i32, (HG_CHUNK, HG_CHUNK), 1)
    tri = jnp.where(c <= r, 1.0, 0.0).astype(bf16)

    def body(ci, st):
        r0 = pl.multiple_of(ci * HG_CHUNK, HG_CHUNK)
        out, st = _hgrn_chunk(q_ref[pl.ds(r0, HG_CHUNK), :], z_ref[pl.ds(r0, HG_CHUNK), :],
                              v_ref[pl.ds(r0, HG_CHUNK), :], lb, st, tri)
        o_ref[pl.ds(r0, HG_CHUNK), :] = out
        return st

    st_ref[...] = lax.fori_loop(0, HG_STEP_ROWS // HG_CHUNK, body, st_ref[...])


def _hgrn_call(layer, proj, lb_logits, B, S):
    N = proj.shape[0]
    R = HG_STEP_ROWS
    spb = S // R

    def col(c0):
        return pl.BlockSpec((R, HG_D), lambda b, h, s, l: (b * spb + s, c0 + h))

    grid_spec = pltpu.PrefetchScalarGridSpec(
        num_scalar_prefetch=1,
        grid=(B, HG_HEADS, spb),
        in_specs=[col(COL_QA), col(COL_FA), col(COL_IA),
                  pl.BlockSpec((DEPTH, HG_D), lambda b, h, s, l: (0, h))],
        out_specs=pl.BlockSpec((R, HG_D), lambda b, h, s, l: (b * spb + s, h)),
        scratch_shapes=[pltpu.VMEM((HG_D, HG_D), f32)],
    )
    return pl.pallas_call(
        _hgrn_kernel,
        grid_spec=grid_spec,
        out_shape=jax.ShapeDtypeStruct((N, HG_W), f32),
        compiler_params=_params(("arbitrary", "arbitrary", "arbitrary")),
        name="hgrn",
    )(layer, proj, proj, proj, lb_logits)


SB_SCALE = 1.0 / math.sqrt(SB_DH)
SB_PAIRS = 4
SB_GROUP = 3
SB_DEAD = -104.0
SB_NEVER = -(1 << 20)


def _sb_kernel(q_ref, k_ref, v_ref, o_ref):
    i = pl.program_id(2)
    T = SB_BLOCK
    P = SB_PAIRS
    lane = lax.broadcasted_iota(i32, (T, T), 1)
    lane2 = lax.broadcasted_iota(i32, (T, 2 * T), 1)
    row2 = lax.broadcasted_iota(i32, (T, 2 * T), 0)
    u = jnp.where((row2 > lane2) | (lane2 >= T), 1.0, 0.0).astype(bf16)
    qms = []
    for p in range(P):
        q2 = q_ref[:, p * T:(p + 1) * T] * SB_SCALE
        qms.append(jnp.concatenate([jnp.where(lane < SB_DH, q2, 0.0), jnp.where(lane >= SB_DH, q2, 0.0)],
                                   axis=0).astype(bf16))
    lane_s = lax.broadcasted_iota(i32, (2 * T, T), 1)
    row_s = lax.broadcasted_iota(i32, (2 * T, T), 0)
    key_minus_query = lane_s - (row_s & (T - 1))

    def trip(state):
        j_hi, carries, accs = state
        carries, accs = list(carries), list(accs)
        chains = [(p, b) for p in range(P) for b in range(SB_GROUP)]
        allowed, ks, vs = [], {}, {}
        for b in range(SB_GROUP):
            j = j_hi - b
            r0 = pl.multiple_of(jnp.maximum(j, 0) * T, T)
            allowed.append(key_minus_query < jnp.where(j >= 0, (i - j) * T, SB_NEVER))
            for p in range(P):
                ks[p, b] = k_ref[pl.ds(r0, T), p * T:(p + 1) * T].astype(bf16)
                vs[p, b] = v_ref[pl.ds(r0, T), p * T:(p + 1) * T].astype(bf16)
        zs = {c: _nt(qms[c[0]], ks[c]) for c in chains}
        lss, his, los = {}, {}, {}
        for c in chains:
            z = zs[c]
            lss[c] = jnp.minimum(z, 0.0) - jnp.log(1.0 + jnp.exp(-jnp.abs(z)))
            his[c], los[c] = _split_bf16(jnp.where(allowed[c[1]], lss[c] - z, 0.0))
        rs = {c: _dot(his[c], u) + _dot(los[c], u) for c in chains}
        ws = {}
        for p, b in chains:
            r = rs[p, b]
            ws[p, b] = jnp.where(allowed[b], jnp.exp(lss[p, b] + r[:, :T] + carries[p]), 0.0).astype(bf16)
            carries[p] = carries[p] + r[:, T:]
        for p, b in chains:
            accs[p] = accs[p] + _dot(ws[p, b], vs[p, b])
        return j_hi - SB_GROUP, tuple(carries), tuple(accs)

    def live(state):
        j_hi, carries, _ = state
        top = carries[0]
        for c in carries[1:]:
            top = jnp.maximum(top, c)
        return (j_hi >= 0) & (jnp.max(top) > SB_DEAD)

    zero = jnp.zeros((2 * T, T), f32)
    _, _, accs = lax.while_loop(live, trip, (i, (zero,) * P, (zero,) * P))
    for p in range(P):
        o_ref[:, p * T:(p + 1) * T] = jnp.where(lane < SB_DH, accs[p][:T], accs[p][T:])


def _sb_call(proj, B, S):
    N = proj.shape[0]
    nq = S // SB_BLOCK
    W = SB_PAIRS * LANES
    return pl.pallas_call(
        _sb_kernel,
        grid=(B, SB_HEADS // (2 * SB_PAIRS), nq),
        in_specs=[pl.BlockSpec((SB_BLOCK, W), lambda b, p, i: (b * nq + i, COL_QB // SB_PAIRS + p)),
                  pl.BlockSpec((S, W), lambda b, p, i: (b, COL_KB // SB_PAIRS + p)),
                  pl.BlockSpec((S, W), lambda b, p, i: (b, COL_VB // SB_PAIRS + p))],
        out_specs=pl.BlockSpec((SB_BLOCK, W), lambda b, p, i: (b * nq + i, p)),
        out_shape=jax.ShapeDtypeStruct((N, SB_W), f32),
        compiler_params=_params(("arbitrary", "arbitrary", "arbitrary")),
        name="sb",
    )(proj, proj, proj)


def _layer_norm(r, g, b):
    mu = jnp.mean(r, axis=-1, keepdims=True)
    d = r - mu
    var = jnp.mean(d * d, axis=-1, keepdims=True)
    return d * lax.rsqrt(var + LN_EPS) * g + b


def _merge_kernel(x_ref, oa_ref, ga_ref, ob_ref, gta0_ref, gta1_ref, gtb0_ref, gtb1_ref,
                  gt1_ref, sh2_ref, sc2_ref, hgn_ref, wua_ref, wub_ref, wo_ref, wpq_ref,
                  lng_ref, lnb_ref, x1_ref, hf_ref, pq_ref):
    oa = oa_ref[...]
    hgn = hgn_ref[...]
    segs = []
    for h in range(HG_HEADS):
        seg = oa[:, h * HG_D:(h + 1) * HG_D]
        ms = jnp.mean(seg * seg, axis=-1, keepdims=True)
        segs.append(seg * lax.rsqrt(ms + RMS_EPS) * hgn[:, h * HG_D:(h + 1) * HG_D])
    ga = ga_ref[...]
    oa_n = jnp.concatenate(segs, axis=-1) * (ga * jax.nn.sigmoid(ga))
    ma = _dot(oa_n.astype(bf16), wua_ref[...])
    mb = _dot(ob_ref[...].astype(bf16), wub_ref[...])
    gate_a = jnp.concatenate([gta0_ref[...], gta1_ref[...]], axis=-1)
    gate_b = jnp.concatenate([gtb0_ref[...], gtb1_ref[...]], axis=-1)
    merged = jax.nn.sigmoid(gate_a) * ma + jax.nn.sigmoid(gate_b) * mb
    y = _dot(merged.astype(bf16), wo_ref[...]) * gt1_ref[...]
    x1 = _layer_norm(DN_ALPHA * x_ref[...] + y, lng_ref[...], lnb_ref[...])
    x1_ref[...] = x1
    hf = x1 * (1.0 + sc2_ref[...]) + sh2_ref[...]
    tm = hf.shape[0]
    for r in range(SUBLANES):
        hf_ref[pl.ds(r, tm, stride=SUBLANES), :] = hf[:, r * LANES:(r + 1) * LANES]
    pq_ref[...] = _dot(hf.astype(bf16), wpq_ref[...])


def _merge_call(x, oa, proj, ob, mod6, hgn, wua, wub, wo, wpq, lng, lnb, S):
    N = x.shape[0]
    tm = 256
    tps = S // tm

    def full(shape):
        return pl.BlockSpec(shape, lambda i: (0,) * len(shape))

    def p512(c):
        return pl.BlockSpec((tm, 512), lambda i: (i, c))

    row = pl.BlockSpec((tm, D_MODEL), lambda i: (i, 0))
    out = jax.ShapeDtypeStruct((N, D_MODEL), f32)
    return pl.pallas_call(
        _merge_kernel,
        grid=(N // tm,),
        in_specs=[row, pl.BlockSpec((tm, HG_W), lambda i: (i, 0)), p512(COL512_GA),
                  pl.BlockSpec((tm, SB_W), lambda i: (i, 0)),
                  p512(COL512_GATE_A), p512(COL512_GATE_A + 1),
                  p512(COL512_GATE_B), p512(COL512_GATE_B + 1),
                  _mod_spec(2, tps), _mod_spec(3, tps), _mod_spec(4, tps),
                  full((1, HG_W)), full((HG_W, D_MODEL)), full((SB_W, D_MODEL)),
                  full((D_MODEL, D_MODEL)), full((D_MODEL, D_MODEL)),
                  full((1, D_MODEL)), full((1, D_MODEL))],
        out_specs=[row, pl.BlockSpec((tm * SUBLANES, LANES), lambda i: (i, 0)), row],
        out_shape=[out, jax.ShapeDtypeStruct((N * SUBLANES, LANES), f32), out],
        compiler_params=_params(("arbitrary",)),
        name="merge",
    )(x, oa, proj, ob, proj, proj, proj, proj, mod6, mod6, mod6, hgn, wua, wub, wo, wpq, lng, lnb)


TOPK_TOKENS = 256


def _top16(s, payload=None):
    R = s.shape[0]
    rid = lax.broadcasted_iota(i32, s.shape, 0).astype(f32)
    vals, ids = [], []
    for _ in range(PEER_TOPK):
        m = jnp.max(s, axis=0, keepdims=True)
        first = jnp.min(jnp.where(s == m, rid, float(R)), axis=0, keepdims=True)
        sel = rid == first
        vals.append(m)
        if payload is None:
            ids.append(first.astype(i32))
        else:
            ids.append(jnp.max(jnp.where(sel, payload, -1), axis=0, keepdims=True))
        s = jnp.where(sel, -jnp.inf, s)
    return vals, ids


def _topk_kernel(pq_ref, k1_ref, k2_ref, idx_ref, gate_ref, idx_t, gate_t):
    H = SUBLANES

    def head(h, _):
        c0 = pl.multiple_of(h * PEER_DKEY, PEER_DKEY)
        qh = pq_ref[:, pl.ds(c0, PEER_DKEY)].astype(bf16)
        s1 = _nt(k1_ref[h], qh)
        s2 = _nt(k2_ref[h], qh)
        v1, i1 = _top16(s1)
        v2, i2 = _top16(s2)
        v1a, i1a = jnp.concatenate(v1, axis=0), jnp.concatenate(i1, axis=0)
        v2a, i2a = jnp.concatenate(v2, axis=0), jnp.concatenate(i2, axis=0)
        cand = ([v1[0] + v2a[:H], v1[0] + v2a[H:]] + [v1[a] + v2a[:H] for a in range(1, H)]
                + [v1a[H:] + v2[0]])
        cidx = ([i1[0] * PEER_NKEYS + i2a[:H], i1[0] * PEER_NKEYS + i2a[H:]]
                + [i1[a] * PEER_NKEYS + i2a[:H] for a in range(1, H)] + [i1a[H:] * PEER_NKEYS + i2[0]])
        tv, ti = _top16(jnp.concatenate(cand, axis=0), jnp.concatenate(cidx, axis=0) * ROW_SUB)
        tva = jnp.concatenate(tv, axis=0)
        e = jnp.exp(tva - tv[0])
        r0 = pl.multiple_of(h * PEER_TOPK, PEER_TOPK)
        gate_t[pl.ds(r0, PEER_TOPK), :] = e / jnp.sum(e, axis=0, keepdims=True)
        idx_t[pl.ds(r0, PEER_TOPK), :] = jnp.concatenate(ti, axis=0)
        return 0

    lax.fori_loop(0, PEER_HEADS, head, 0)
    idx_ref[...] = idx_t[...].T
    gate_ref[...] = gate_t[...].T


def _topk_call(pq, k1p, k2p):
    N = pq.shape[0]
    T = TOPK_TOKENS
    keys = pl.BlockSpec((PEER_HEADS, PEER_NKEYS, PEER_DKEY), lambda i: (0, 0, 0))
    return pl.pallas_call(
        _topk_kernel,
        grid=(N // T,),
        in_specs=[pl.BlockSpec((T, D_MODEL), lambda i: (i, 0)), keys, keys],
        out_specs=[pl.BlockSpec((T, PEER_PAIRS), lambda i: (i, 0)),
                   pl.BlockSpec((T, PEER_PAIRS), lambda i: (i, 0))],
        out_shape=[jax.ShapeDtypeStruct((N, PEER_PAIRS), i32),
                   jax.ShapeDtypeStruct((N, PEER_PAIRS), f32)],
        scratch_shapes=[pltpu.VMEM((PEER_PAIRS, T), i32), pltpu.VMEM((PEER_PAIRS, T), f32)],
        compiler_params=_params(("arbitrary",)),
        name="topk",
    )(pq, k1p, k2p)


PEER_TOKENS = 64
PEER_GROUP = SUBLANES
STAGE_ROWS = PEER_PAIRS * ROW_SUB
STAGE_COLS = 2 * STAGE_ROWS
INV_SQRT2 = 1.0 / math.sqrt(2.0)


def _gather_rows(idx_s, k0, tab_ref, stage_ref):
    for k in range(PEER_PAIRS):
        e4 = pl.multiple_of(idx_s[k0 + k], ROW_SUB)
        stage_ref[k * ROW_SUB:(k + 1) * ROW_SUB, :] = tab_ref[pl.ds(e4, ROW_SUB), :]


def _piece_diag():
    sub = lax.broadcasted_iota(i32, (SUBLANES, STAGE_COLS), 0)
    col = lax.broadcasted_iota(i32, (SUBLANES, STAGE_COLS), 1)
    return (col & (SUBLANES - 1)) == 2 * (sub & (ROW_SUB - 1)) + lax.shift_right_logical(sub, 2)


def _peer_u_kernel(idx_s, hf_ref, g_ref, fold_ref, tab_ref, o_ref, stage, zs_ref):
    T = o_ref.shape[0]
    G = PEER_GROUP
    diag = _piece_diag()

    def group(gi, _):
        g0 = pl.multiple_of(gi * G, G)
        for t in range(G):
            _gather_rows(idx_s, (g0 + t) * PEER_PAIRS, tab_ref, stage)
            h_hi, h_lo = _split_bf16(hf_ref[pl.ds(pl.multiple_of((g0 + t) * SUBLANES, SUBLANES), SUBLANES), :])
            z = _nt(jnp.concatenate([h_hi, h_lo], axis=0), pltpu.bitcast(stage[...], bf16))
            zs_ref[t:t + 1, :] = jnp.sum(jnp.where(diag, z[:SUBLANES] + z[SUBLANES:], 0.0), axis=0, keepdims=True)
        zs_hi, zs_lo = _split_bf16(zs_ref[...])
        dots = _dot(zs_hi, fold_ref[...]) + _dot(zs_lo, fold_ref[...])
        act = 0.5 * dots * (1.0 + lax.erf(dots * INV_SQRT2))
        o_ref[pl.ds(g0, G), :] = act * g_ref[pl.ds(g0, G), :]
        return 0

    lax.fori_loop(0, T // G, group, 0)


def _peer_v_kernel(idx_s, w_ref, spread_ref, tab_ref, o_ref, stage):
    T = w_ref.shape[0]
    G = PEER_GROUP
    diag = _piece_diag()

    def group(gi, _):
        g0 = pl.multiple_of(gi * G, G)
        w8 = _dot(w_ref[pl.ds(g0, G), :].astype(bf16), spread_ref[...])
        for t in range(G):
            _gather_rows(idx_s, (g0 + t) * PEER_PAIRS, tab_ref, stage)
            wexp = jnp.where(diag, jnp.broadcast_to(w8[t:t + 1, :], diag.shape), 0.0).astype(bf16)
            r0 = pl.multiple_of((g0 + t) * SUBLANES, SUBLANES)
            o_ref[pl.ds(r0, SUBLANES), :] = _dot(wexp, pltpu.bitcast(stage[...], bf16))
        return 0

    lax.fori_loop(0, T // G, group, 0)


def _table_spec():
    return pl.BlockSpec((PEER_N * ROW_SUB, LANES), lambda i: (0, 0), pipeline_mode=pl.Buffered(1))


def _smem_flat(T):
    return pl.BlockSpec((T * PEER_PAIRS,), lambda i: (i,), memory_space=pltpu.SMEM)


def _pair_pieces():
    k = lax.broadcasted_iota(i32, (PEER_PAIRS, STAGE_COLS), 0)
    c = lax.broadcasted_iota(i32, (PEER_PAIRS, STAGE_COLS), 1)
    return (c // SUBLANES == k).astype(bf16)


def _peer_u_call(idx4, hf8, gates, u_pk):
    N = idx4.shape[0]
    T = PEER_TOKENS
    rows = pl.BlockSpec((T, PEER_PAIRS), lambda i: (i, 0))
    return pl.pallas_call(
        _peer_u_kernel,
        grid=(N // T,),
        in_specs=[_smem_flat(T),
                  pl.BlockSpec((T * SUBLANES, LANES), lambda i: (i, 0)),
                  rows, pl.BlockSpec((STAGE_COLS, PEER_PAIRS), lambda i: (0, 0)), _table_spec()],
        out_specs=rows,
        out_shape=jax.ShapeDtypeStruct((N, PEER_PAIRS), f32),
        scratch_shapes=[pltpu.VMEM((STAGE_ROWS, LANES), i32), pltpu.VMEM((PEER_GROUP, STAGE_COLS), f32)],
        compiler_params=_params(("arbitrary",)),
        name="peer_u",
    )(idx4.reshape(-1), hf8, gates, _pair_pieces().T, u_pk)


def _peer_v_call(idx4, wgt, v_pk):
    N = idx4.shape[0]
    T = PEER_TOKENS
    return pl.pallas_call(
        _peer_v_kernel,
        grid=(N // T,),
        in_specs=[_smem_flat(T), pl.BlockSpec((T, PEER_PAIRS), lambda i: (i, 0)),
                  pl.BlockSpec((PEER_PAIRS, STAGE_COLS), lambda i: (0, 0)), _table_spec()],
        out_specs=pl.BlockSpec((T * SUBLANES, LANES), lambda i: (i, 0)),
        out_shape=jax.ShapeDtypeStruct((N * SUBLANES, LANES), f32),
        scratch_shapes=[pltpu.VMEM((STAGE_ROWS, LANES), i32)],
        compiler_params=_params(("arbitrary",)),
        name="peer_v",
    )(idx4.reshape(-1), wgt, _pair_pieces(), v_pk)


def _ln_kernel(x_ref, y_ref, gt_ref, g_ref, b_ref, o_ref):
    tm = x_ref.shape[0]
    y = jnp.concatenate([y_ref[pl.ds(r, tm, stride=SUBLANES), :] for r in range(SUBLANES)], axis=-1)
    o_ref[...] = _layer_norm(DN_ALPHA * x_ref[...] + y * gt_ref[...], g_ref[...], b_ref[...])


def _ln_call(x1, y8, mod6, lng, lnb, S):
    N = x1.shape[0]
    tm = 512
    row = pl.BlockSpec((tm, D_MODEL), lambda i: (i, 0))
    vec = pl.BlockSpec((1, D_MODEL), lambda i: (0, 0))
    return pl.pallas_call(
        _ln_kernel,
        grid=(N // tm,),
        in_specs=[row, pl.BlockSpec((tm * SUBLANES, LANES), lambda i: (i, 0)), _mod_spec(5, S // tm), vec, vec],
        out_specs=row,
        out_shape=jax.ShapeDtypeStruct((N, D_MODEL), f32),
        compiler_params=_params(("arbitrary",)),
        name="ln",
    )(x1, y8, mod6, lng, lnb)


def _pack_table(t):
    tb = lax.bitcast_convert_type(t.astype(bf16), jnp.uint16).astype(jnp.uint32)
    word = tb[:, :ROW_WORDS] | (tb[:, ROW_WORDS:] << 16)
    return lax.bitcast_convert_type(word, i32).reshape(PEER_N * ROW_SUB, LANES)


def _pad_keys(sub_keys_l):
    half = PEER_DKEY // 2
    z = jnp.zeros((PEER_HEADS, PEER_NKEYS, half), f32)
    k1 = jnp.concatenate([sub_keys_l[0], z], axis=-1).astype(bf16)
    k2 = jnp.concatenate([z, sub_keys_l[1]], axis=-1).astype(bf16)
    return k1, k2


def _layer(x, c, lb_logits, B, S, layer, p):
    (w_ada, b_ada, w_in, hgn, wua, wub, wo, wpq, sub_keys, pu, pv, ln_g, ln_b) = p
    mod6 = _mod_call(c, w_ada, b_ada.reshape(1, -1)).reshape(B, 6, 1, D_MODEL)
    proj = _proj_call(x, mod6, w_in.astype(bf16), S)
    oa = _hgrn_call(layer.reshape(1), proj, lb_logits, B, S)
    ob = _sb_call(proj, B, S)
    x1, hf8, pq = _merge_call(x, oa, proj, ob, mod6, hgn.reshape(1, -1), wua.astype(bf16),
                              wub.astype(bf16), wo.astype(bf16), wpq.astype(bf16),
                              ln_g[0:1], ln_b[0:1], S)
    k1p, k2p = _pad_keys(sub_keys)
    idx4, gates = _topk_call(pq, k1p, k2p)
    wgt = _peer_u_call(idx4, hf8, gates, _pack_table(pu))
    y8 = _peer_v_call(idx4, wgt, _pack_table(pv))
    return _ln_call(x1, y8, mod6, ln_g[1:2], ln_b[1:2], S)


def kernel(x, c, w_ada, b_ada, w_in, lb_logits, hg_norm_g, w_up_a, w_up_b, w_o, w_pq, sub_keys,
           peer_u, peer_v, ln_g, ln_b):
    B, S, _ = x.shape
    xs = (jnp.arange(DEPTH, dtype=i32), w_ada, b_ada, w_in, hg_norm_g, w_up_a, w_up_b, w_o, w_pq,
          sub_keys, peer_u, peer_v, ln_g, ln_b)

    def step(xc, per_layer):
        return _layer(xc, c, lb_logits, B, S, per_layer[0], per_layer[1:]), None

    out, _ = lax.scan(step, x.reshape(B * S, D_MODEL), xs)
    return out.reshape(B, S, D_MODEL)
```

```python
import dataclasses
import functools
import math

import jax
import jax.numpy as jnp
from jax import lax
from jax.experimental import pallas as pl
from jax.experimental.pallas import tpu as pltpu
from jax.experimental.pallas import tpu_sc as plsc

f32 = jnp.float32
bf16 = jnp.bfloat16
i32 = jnp.int32

D_MODEL = 1024
DEPTH = 4
HG_HEADS = 4
HG_D = 128
HG_W = HG_HEADS * HG_D
SB_HEADS = 8
SB_DH = 64
SB_W = SB_HEADS * SB_DH
SB_BLOCK = 128
PEER_HEADS = 8
PEER_NKEYS = 128
PEER_N = PEER_NKEYS * PEER_NKEYS
PEER_DKEY = 128
PEER_TOPK = 16
PEER_PAIRS = PEER_HEADS * PEER_TOPK
IN_WIDTH = 4 * HG_W + 3 * SB_W + 2 * D_MODEL
DN_ALPHA = (2.0 * DEPTH) ** 0.25
LN_EPS = 1e-5
RMS_EPS = 1e-6

LANES = 128
SUBLANES = 8
ROW_WORDS = D_MODEL // 2
ROW_SUB = ROW_WORDS // LANES
VMEM_LIMIT = 48 * 1024 * 1024

COL_QA, COL_FA, COL_IA = 0, 4, 8
COL_QB, COL_KB, COL_VB = 16, 20, 24
COL512_GA, COL512_GATE_A, COL512_GATE_B = 3, 7, 9


def _nt(a, b):
    return lax.dot_general(a, b, (((1,), (1,)), ((), ())), preferred_element_type=f32)


def _tn(a, b):
    return lax.dot_general(a, b, (((0,), (0,)), ((), ())), preferred_element_type=f32)


def _dot(a, b):
    return jnp.dot(a, b, preferred_element_type=f32)


def _split_bf16(a):
    hi = a.astype(bf16)
    lo = (a - hi.astype(f32)).astype(bf16)
    return hi, lo


def _params(sem):
    return pltpu.CompilerParams(dimension_semantics=sem, vmem_limit_bytes=VMEM_LIMIT)


def _mod_kernel(c_ref, w_ref, b_ref, o_ref):
    c = c_ref[...]
    cond = (c * jax.nn.sigmoid(c)).astype(bf16)
    o_ref[...] = _dot(cond, w_ref[...].astype(bf16)) + b_ref[...]


def _mod_call(c, w_ada_l, b_ada_l):
    B = c.shape[0]
    tn = 1536
    return pl.pallas_call(
        _mod_kernel,
        grid=(6 * D_MODEL // tn,),
        in_specs=[pl.BlockSpec((B, D_MODEL), lambda j: (0, 0)),
                  pl.BlockSpec((D_MODEL, tn), lambda j: (0, j)),
                  pl.BlockSpec((1, tn), lambda j: (0, j))],
        out_specs=pl.BlockSpec((B, tn), lambda j: (0, j)),
        out_shape=jax.ShapeDtypeStruct((B, 6 * D_MODEL), f32),
        compiler_params=_params(("arbitrary",)),
        name="mod",
    )(c, w_ada_l, b_ada_l)


def _mod_spec(which, tiles_per_seq):
    return pl.BlockSpec((None, None, 1, D_MODEL), lambda i: (i // tiles_per_seq, which, 0, 0))


def _proj_kernel(x_ref, sh_ref, sc_ref, w_ref, o_ref):
    hm = (x_ref[...] * (1.0 + sc_ref[...]) + sh_ref[...]).astype(bf16)
    for n0 in range(0, IN_WIDTH, 512):
        o_ref[:, n0:n0 + 512] = _dot(hm, w_ref[:, n0:n0 + 512])


def _proj_call(x, mod6, w_in_bf, S):
    N = x.shape[0]
    tm = 256
    tps = S // tm
    return pl.pallas_call(
        _proj_kernel,
        grid=(N // tm,),
        in_specs=[pl.BlockSpec((tm, D_MODEL), lambda i: (i, 0)),
                  _mod_spec(0, tps), _mod_spec(1, tps),
                  pl.BlockSpec((D_MODEL, IN_WIDTH), lambda i: (0, 0))],
        out_specs=pl.BlockSpec((tm, IN_WIDTH), lambda i: (i, 0)),
        out_shape=jax.ShapeDtypeStruct((N, IN_WIDTH), f32),
        compiler_params=_params(("arbitrary",)),
        name="proj",
    )(x, mod6, mod6, w_in_bf)


HG_CHUNK = 128
HG_SUB = 16
HG_STEP_ROWS = 512


def _hgrn_chunk(q, z, v, lb, st, tri):
    one_m_lb = 1.0 - lb
    g = jnp.log(lb + one_m_lb * jax.nn.sigmoid(z))
    k = one_m_lb * jax.nn.sigmoid(-z)
    g_hi, g_lo = _split_bf16(g)
    b = _dot(tri, g_hi) + _dot(tri, g_lo)
    b_end = b[HG_CHUNK - 1:HG_CHUNK, :]
    inter = _nt((q * jnp.exp(b)).astype(bf16), st.astype(bf16))
    v_bf = v.astype(bf16)
    row_c = lax.broadcasted_iota(i32, (HG_CHUNK, HG_D), 0)
    row_s = lax.broadcasted_iota(i32, (HG_SUB, HG_D), 0)
    blocks = []
    for sub in range(HG_CHUNK // HG_SUB):
        r0 = sub * HG_SUB
        bs = b[r0:r0 + HG_SUB]
        qs = q[r0:r0 + HG_SUB]
        ks = k[r0:r0 + HG_SUB]
        vs = v[r0:r0 + HG_SUB]
        rows = []
        for t in range(HG_SUB):
            m = row_s <= t
            e = jnp.where(m, jnp.exp(jnp.where(m, bs[t:t + 1] - bs, 0.0)), 0.0)
            p = (qs[t:t + 1] * ks) * e
            srow = jnp.sum(p, axis=-1, keepdims=True)
            rows.append(jnp.sum(srow * vs, axis=0, keepdims=True))
        o_sub = jnp.concatenate(rows, axis=0)
        if sub > 0:
            bref = b[r0 - 1:r0, :]
            qi = (qs * jnp.exp(bs - bref)).astype(bf16)
            past = row_c < r0
            ki = jnp.where(past, k * jnp.exp(jnp.where(past, bref - b, 0.0)), 0.0).astype(bf16)
            o_sub = o_sub + _dot(_nt(qi, ki).astype(bf16), v_bf)
        blocks.append(o_sub)
    intra = jnp.concatenate(blocks, axis=0)
    kd = (k * jnp.exp(b_end - b)).astype(bf16)
    st_new = st * jnp.exp(b_end) + _tn(v_bf, kd)
    return inter + intra, st_new


def _hgrn_kernel(l_ref, q_ref, z_ref, v_ref, lbl_ref, o_ref, st_ref):
    @pl.when(pl.program_id(2) == 0)
    def _():
        st_ref[...] = jnp.zeros_like(st_ref)

    logits = lbl_ref[...]
    e = jnp.exp(logits - jnp.max(logits, axis=0, keepdims=True))
    p = e / jnp.sum(e, axis=0, keepdims=True)
    rid = lax.broadcasted_iota(i32, p.shape, 0)
    l = l_ref[0]
    lb = jnp.sum(jnp.where((rid >= 1) & (rid <= l), p, 0.0), axis=0, keepdims=True)

    r = lax.broadcasted_iota(i32, (HG_CHUNK, HG_CHUNK), 0)
    c = lax.broadcasted_iota(i32, (HG_CHUNK, HG_CHUNK), 1)
    tri = jnp.where(c <= r, 1.0, 0.0).astype(bf16)

    def body(ci, st):
        r0 = pl.multiple_of(ci * HG_CHUNK, HG_CHUNK)
        out, st = _hgrn_chunk(q_ref[pl.ds(r0, HG_CHUNK), :], z_ref[pl.ds(r0, HG_CHUNK), :],
                              v_ref[pl.ds(r0, HG_CHUNK), :], lb, st, tri)
        o_ref[pl.ds(r0, HG_CHUNK), :] = out
        return st

    st_ref[...] = lax.fori_loop(0, HG_STEP_ROWS // HG_CHUNK, body, st_ref[...])


def _hgrn_call(layer, proj, lb_logits, B, S):
    N = proj.shape[0]
    R = HG_STEP_ROWS
    spb = S // R

    def col(c0):
        return pl.BlockSpec((R, HG_D), lambda b, h, s, l: (b * spb + s, c0 + h))

    grid_spec = pltpu.PrefetchScalarGridSpec(
        num_scalar_prefetch=1,
        grid=(B, HG_HEADS, spb),
        in_specs=[col(COL_QA), col(COL_FA), col(COL_IA),
                  pl.BlockSpec((DEPTH, HG_D), lambda b, h, s, l: (0, h))],
        out_specs=pl.BlockSpec((R, HG_D), lambda b, h, s, l: (b * spb + s, h)),
        scratch_shapes=[pltpu.VMEM((HG_D, HG_D), f32)],
    )
    return pl.pallas_call(
        _hgrn_kernel,
        grid_spec=grid_spec,
        out_shape=jax.ShapeDtypeStruct((N, HG_W), f32),
        compiler_params=_params(("arbitrary", "arbitrary", "arbitrary")),
        name="hgrn",
    )(layer, proj, proj, proj, lb_logits)


SB_SCALE = 1.0 / math.sqrt(SB_DH)
SB_PAIRS = 4
SB_GROUP = 3
SB_DEAD = -104.0
SB_NEVER = -(1 << 20)


def _sb_kernel(q_ref, k_ref, v_ref, o_ref):
    i = pl.program_id(2)
    T = SB_BLOCK
    P = SB_PAIRS
    lane = lax.broadcasted_iota(i32, (T, T), 1)
    lane2 = lax.broadcasted_iota(i32, (T, 2 * T), 1)
    row2 = lax.broadcasted_iota(i32, (T, 2 * T), 0)
    u = jnp.where((row2 > lane2) | (lane2 >= T), 1.0, 0.0).astype(bf16)
    qms = []
    for p in range(P):
        q2 = q_ref[:, p * T:(p + 1) * T] * SB_SCALE
        qms.append(jnp.concatenate([jnp.where(lane < SB_DH, q2, 0.0), jnp.where(lane >= SB_DH, q2, 0.0)],
                                   axis=0).astype(bf16))
    lane_s = lax.broadcasted_iota(i32, (2 * T, T), 1)
    row_s = lax.broadcasted_iota(i32, (2 * T, T), 0)
    key_minus_query = lane_s - (row_s & (T - 1))

    def trip(state):
        j_hi, carries, accs = state
        carries, accs = list(carries), list(accs)
        chains = [(p, b) for p in range(P) for b in range(SB_GROUP)]
        allowed, ks, vs = [], {}, {}
        for b in range(SB_GROUP):
            j = j_hi - b
            r0 = pl.multiple_of(jnp.maximum(j, 0) * T, T)
            allowed.append(key_minus_query < jnp.where(j >= 0, (i - j) * T, SB_NEVER))
            for p in range(P):
                ks[p, b] = k_ref[pl.ds(r0, T), p * T:(p + 1) * T].astype(bf16)
                vs[p, b] = v_ref[pl.ds(r0, T), p * T:(p + 1) * T].astype(bf16)
        zs = {c: _nt(qms[c[0]], ks[c]) for c in chains}
        lss, his, los = {}, {}, {}
        for c in chains:
            z = zs[c]
            lss[c] = jnp.minimum(z, 0.0) - jnp.log(1.0 + jnp.exp(-jnp.abs(z)))
            his[c], los[c] = _split_bf16(jnp.where(allowed[c[1]], lss[c] - z, 0.0))
        rs = {c: _dot(his[c], u) + _dot(los[c], u) for c in chains}
        ws = {}
        for p, b in chains:
            r = rs[p, b]
            ws[p, b] = jnp.where(allowed[b], jnp.exp(lss[p, b] + r[:, :T] + carries[p]), 0.0).astype(bf16)
            carries[p] = carries[p] + r[:, T:]
        for p, b in chains:
            accs[p] = accs[p] + _dot(ws[p, b], vs[p, b])
        return j_hi - SB_GROUP, tuple(carries), tuple(accs)

    def live(state):
        j_hi, carries, _ = state
        top = carries[0]
        for c in carries[1:]:
            top = jnp.maximum(top, c)
        return (j_hi >= 0) & (jnp.max(top) > SB_DEAD)

    zero = jnp.zeros((2 * T, T), f32)
    _, _, accs = lax.while_loop(live, trip, (i, (zero,) * P, (zero,) * P))
    for p in range(P):
        o_ref[:, p * T:(p + 1) * T] = jnp.where(lane < SB_DH, accs[p][:T], accs[p][T:])


def _sb_call(proj, B, S):
    N = proj.shape[0]
    nq = S // SB_BLOCK
    W = SB_PAIRS * LANES
    return pl.pallas_call(
        _sb_kernel,
        grid=(B, SB_HEADS // (2 * SB_PAIRS), nq),
        in_specs=[pl.BlockSpec((SB_BLOCK, W), lambda b, p, i: (b * nq + i, COL_QB // SB_PAIRS + p)),
                  pl.BlockSpec((S, W), lambda b, p, i: (b, COL_KB // SB_PAIRS + p)),
                  pl.BlockSpec((S, W), lambda b, p, i: (b, COL_VB // SB_PAIRS + p))],
        out_specs=pl.BlockSpec((SB_BLOCK, W), lambda b, p, i: (b * nq + i, p)),
        out_shape=jax.ShapeDtypeStruct((N, SB_W), f32),
        compiler_params=_params(("arbitrary", "arbitrary", "arbitrary")),
        name="sb",
    )(proj, proj, proj)


def _layer_norm(r, g, b):
    mu = jnp.mean(r, axis=-1, keepdims=True)
    d = r - mu
    var = jnp.mean(d * d, axis=-1, keepdims=True)
    return d * lax.rsqrt(var + LN_EPS) * g + b


def _merge_kernel(x_ref, oa_ref, ga_ref, ob_ref, gta0_ref, gta1_ref, gtb0_ref, gtb1_ref,
                  gt1_ref, sh2_ref, sc2_ref, hgn_ref, wua_ref, wub_ref, wo_ref, wpq_ref,
                  lng_ref, lnb_ref, x1_ref, hf_ref, pq_ref):
    oa = oa_ref[...]
    hgn = hgn_ref[...]
    segs = []
    for h in range(HG_HEADS):
        seg = oa[:, h * HG_D:(h + 1) * HG_D]
        ms = jnp.mean(seg * seg, axis=-1, keepdims=True)
        segs.append(seg * lax.rsqrt(ms + RMS_EPS) * hgn[:, h * HG_D:(h + 1) * HG_D])
    ga = ga_ref[...]
    oa_n = jnp.concatenate(segs, axis=-1) * (ga * jax.nn.sigmoid(ga))
    ma = _dot(oa_n.astype(bf16), wua_ref[...])
    mb = _dot(ob_ref[...].astype(bf16), wub_ref[...])
    gate_a = jnp.concatenate([gta0_ref[...], gta1_ref[...]], axis=-1)
    gate_b = jnp.concatenate([gtb0_ref[...], gtb1_ref[...]], axis=-1)
    merged = jax.nn.sigmoid(gate_a) * ma + jax.nn.sigmoid(gate_b) * mb
    y = _dot(merged.astype(bf16), wo_ref[...]) * gt1_ref[...]
    x1 = _layer_norm(DN_ALPHA * x_ref[...] + y, lng_ref[...], lnb_ref[...])
    x1_ref[...] = x1
    hf = x1 * (1.0 + sc2_ref[...]) + sh2_ref[...]
    tm = hf.shape[0]
    for r in range(SUBLANES):
        hf_ref[pl.ds(r, tm, stride=SUBLANES), :] = hf[:, r * LANES:(r + 1) * LANES]
    pq_ref[...] = _dot(hf.astype(bf16), wpq_ref[...])


def _merge_call(x, oa, proj, ob, mod6, hgn, wua, wub, wo, wpq, lng, lnb, S):
    N = x.shape[0]
    tm = 256
    tps = S // tm

    def full(shape):
        return pl.BlockSpec(shape, lambda i: (0,) * len(shape))

    def p512(c):
        return pl.BlockSpec((tm, 512), lambda i: (i, c))

    row = pl.BlockSpec((tm, D_MODEL), lambda i: (i, 0))
    out = jax.ShapeDtypeStruct((N, D_MODEL), f32)
    return pl.pallas_call(
        _merge_kernel,
        grid=(N // tm,),
        in_specs=[row, pl.BlockSpec((tm, HG_W), lambda i: (i, 0)), p512(COL512_GA),
                  pl.BlockSpec((tm, SB_W), lambda i: (i, 0)),
                  p512(COL512_GATE_A), p512(COL512_GATE_A + 1),
                  p512(COL512_GATE_B), p512(COL512_GATE_B + 1),
                  _mod_spec(2, tps), _mod_spec(3, tps), _mod_spec(4, tps),
                  full((1, HG_W)), full((HG_W, D_MODEL)), full((SB_W, D_MODEL)),
                  full((D_MODEL, D_MODEL)), full((D_MODEL, D_MODEL)),
                  full((1, D_MODEL)), full((1, D_MODEL))],
        out_specs=[row, pl.BlockSpec((tm * SUBLANES, LANES), lambda i: (i, 0)), row],
        out_shape=[out, jax.ShapeDtypeStruct((N * SUBLANES, LANES), f32), out],
        compiler_params=_params(("arbitrary",)),
        name="merge",
    )(x, oa, proj, ob, proj, proj, proj, proj, mod6, mod6, mod6, hgn, wua, wub, wo, wpq, lng, lnb)


TOPK_TOKENS = 256


def _top16(s, payload=None):
    R = s.shape[0]
    rid = lax.broadcasted_iota(i32, s.shape, 0).astype(f32)
    vals, ids = [], []
    for _ in range(PEER_TOPK):
        m = jnp.max(s, axis=0, keepdims=True)
        first = jnp.min(jnp.where(s == m, rid, float(R)), axis=0, keepdims=True)
        sel = rid == first
        vals.append(m)
        if payload is None:
            ids.append(first.astype(i32))
        else:
            ids.append(jnp.max(jnp.where(sel, payload, -1), axis=0, keepdims=True))
        s = jnp.where(sel, -jnp.inf, s)
    return vals, ids


def _topk_kernel(pq_ref, k1_ref, k2_ref, idx_ref, gate_ref, idx_t, gate_t):
    H = SUBLANES

    def head(h, _):
        c0 = pl.multiple_of(h * PEER_DKEY, PEER_DKEY)
        qh = pq_ref[:, pl.ds(c0, PEER_DKEY)].astype(bf16)
        s1 = _nt(k1_ref[h], qh)
        s2 = _nt(k2_ref[h], qh)
        v1, i1 = _top16(s1)
        v2, i2 = _top16(s2)
        v1a, i1a = jnp.concatenate(v1, axis=0), jnp.concatenate(i1, axis=0)
        v2a, i2a = jnp.concatenate(v2, axis=0), jnp.concatenate(i2, axis=0)
        cand = ([v1[0] + v2a[:H], v1[0] + v2a[H:]] + [v1[a] + v2a[:H] for a in range(1, H)]
                + [v1a[H:] + v2[0]])
        cidx = ([i1[0] * PEER_NKEYS + i2a[:H], i1[0] * PEER_NKEYS + i2a[H:]]
                + [i1[a] * PEER_NKEYS + i2a[:H] for a in range(1, H)] + [i1a[H:] * PEER_NKEYS + i2[0]])
        tv, ti = _top16(jnp.concatenate(cand, axis=0), jnp.concatenate(cidx, axis=0) * ROW_SUB)
        tva = jnp.concatenate(tv, axis=0)
        e = jnp.exp(tva - tv[0])
        r0 = pl.multiple_of(h * PEER_TOPK, PEER_TOPK)
        gate_t[pl.ds(r0, PEER_TOPK), :] = e / jnp.sum(e, axis=0, keepdims=True)
        idx_t[pl.ds(r0, PEER_TOPK), :] = jnp.concatenate(ti, axis=0)
        return 0

    lax.fori_loop(0, PEER_HEADS, head, 0)
    idx_ref[...] = idx_t[...].T
    gate_ref[...] = gate_t[...].T


def _topk_call(pq, k1p, k2p):
    N = pq.shape[0]
    T = TOPK_TOKENS
    keys = pl.BlockSpec((PEER_HEADS, PEER_NKEYS, PEER_DKEY), lambda i: (0, 0, 0))
    return pl.pallas_call(
        _topk_kernel,
        grid=(N // T,),
        in_specs=[pl.BlockSpec((T, D_MODEL), lambda i: (i, 0)), keys, keys],
        out_specs=[pl.BlockSpec((T, PEER_PAIRS), lambda i: (i, 0)),
                   pl.BlockSpec((T, PEER_PAIRS), lambda i: (i, 0))],
        out_shape=[jax.ShapeDtypeStruct((N, PEER_PAIRS), i32),
                   jax.ShapeDtypeStruct((N, PEER_PAIRS), f32)],
        scratch_shapes=[pltpu.VMEM((PEER_PAIRS, T), i32), pltpu.VMEM((PEER_PAIRS, T), f32)],
        compiler_params=_params(("arbitrary",)),
        name="topk",
    )(pq, k1p, k2p)


PEER_TOKENS = 64
PEER_GROUP = SUBLANES
STAGE_ROWS = PEER_PAIRS * ROW_SUB
STAGE_COLS = 2 * STAGE_ROWS
INV_SQRT2 = 1.0 / math.sqrt(2.0)


def _gather_rows(idx_s, k0, tab_ref, stage_ref):
    for k in range(PEER_PAIRS):
        e4 = pl.multiple_of(idx_s[k0 + k], ROW_SUB)
        stage_ref[k * ROW_SUB:(k + 1) * ROW_SUB, :] = tab_ref[pl.ds(e4, ROW_SUB), :]


def _piece_diag():
    sub = lax.broadcasted_iota(i32, (SUBLANES, STAGE_COLS), 0)
    col = lax.broadcasted_iota(i32, (SUBLANES, STAGE_COLS), 1)
    return (col & (SUBLANES - 1)) == 2 * (sub & (ROW_SUB - 1)) + lax.shift_right_logical(sub, 2)


def _peer_u_kernel(idx_s, hf_ref, g_ref, fold_ref, tab_ref, o_ref, stage, zs_ref):
    T = o_ref.shape[0]
    G = PEER_GROUP
    diag = _piece_diag()

    def group(gi, _):
        g0 = pl.multiple_of(gi * G, G)
        for t in range(G):
            _gather_rows(idx_s, (g0 + t) * PEER_PAIRS, tab_ref, stage)
            h_hi, h_lo = _split_bf16(hf_ref[pl.ds(pl.multiple_of((g0 + t) * SUBLANES, SUBLANES), SUBLANES), :])
            z = _nt(jnp.concatenate([h_hi, h_lo], axis=0), pltpu.bitcast(stage[...], bf16))
            zs_ref[t:t + 1, :] = jnp.sum(jnp.where(diag, z[:SUBLANES] + z[SUBLANES:], 0.0), axis=0, keepdims=True)
        zs_hi, zs_lo = _split_bf16(zs_ref[...])
        dots = _dot(zs_hi, fold_ref[...]) + _dot(zs_lo, fold_ref[...])
        act = 0.5 * dots * (1.0 + lax.erf(dots * INV_SQRT2))
        o_ref[pl.ds(g0, G), :] = act * g_ref[pl.ds(g0, G), :]
        return 0

    lax.fori_loop(0, T // G, group, 0)


def _peer_v_kernel(idx_s, w_ref, spread_ref, tab_ref, o_ref, stage):
    T = w_ref.shape[0]
    G = PEER_GROUP
    diag = _piece_diag()

    def group(gi, _):
        g0 = pl.multiple_of(gi * G, G)
        w8 = _dot(w_ref[pl.ds(g0, G), :].astype(bf16), spread_ref[...])
        for t in range(G):
            _gather_rows(idx_s, (g0 + t) * PEER_PAIRS, tab_ref, stage)
            wexp = jnp.where(diag, jnp.broadcast_to(w8[t:t + 1, :], diag.shape), 0.0).astype(bf16)
            r0 = pl.multiple_of((g0 + t) * SUBLANES, SUBLANES)
            o_ref[pl.ds(r0, SUBLANES), :] = _dot(wexp, pltpu.bitcast(stage[...], bf16))
        return 0

    lax.fori_loop(0, T // G, group, 0)


def _table_spec():
    return pl.BlockSpec((PEER_N * ROW_SUB, LANES), lambda i: (0, 0), pipeline_mode=pl.Buffered(1))


def _smem_flat(T, first):
    return pl.BlockSpec((T * PEER_PAIRS,), lambda i: (i + first,), memory_space=pltpu.SMEM)


def _pair_pieces():
    k = lax.broadcasted_iota(i32, (PEER_PAIRS, STAGE_COLS), 0)
    c = lax.broadcasted_iota(i32, (PEER_PAIRS, STAGE_COLS), 1)
    return (c // SUBLANES == k).astype(bf16)


def _peer_u_call(idx4, hf8, gates, u_pk, start, count):
    T = PEER_TOKENS
    first = start // T
    return pl.pallas_call(
        _peer_u_kernel,
        grid=(count // T,),
        in_specs=[_smem_flat(T, first),
                  pl.BlockSpec((T * SUBLANES, LANES), lambda i: (i + first, 0)),
                  pl.BlockSpec((T, PEER_PAIRS), lambda i: (i + first, 0)),
                  pl.BlockSpec((STAGE_COLS, PEER_PAIRS), lambda i: (0, 0)), _table_spec()],
        out_specs=pl.BlockSpec((T, PEER_PAIRS), lambda i: (i, 0)),
        out_shape=jax.ShapeDtypeStruct((count, PEER_PAIRS), f32),
        scratch_shapes=[pltpu.VMEM((STAGE_ROWS, LANES), i32), pltpu.VMEM((PEER_GROUP, STAGE_COLS), f32)],
        compiler_params=_params(("arbitrary",)),
        name="peer_u",
    )(idx4.reshape(-1), hf8, gates, _pair_pieces().T, u_pk)


def _peer_v_call(idx4, wgt, v_pk, start):
    count = wgt.shape[0]
    T = PEER_TOKENS
    return pl.pallas_call(
        _peer_v_kernel,
        grid=(count // T,),
        in_specs=[_smem_flat(T, start // T), pl.BlockSpec((T, PEER_PAIRS), lambda i: (i, 0)),
                  pl.BlockSpec((PEER_PAIRS, STAGE_COLS), lambda i: (0, 0)), _table_spec()],
        out_specs=pl.BlockSpec((T * SUBLANES, LANES), lambda i: (i, 0)),
        out_shape=jax.ShapeDtypeStruct((count * SUBLANES, LANES), f32),
        scratch_shapes=[pltpu.VMEM((STAGE_ROWS, LANES), i32)],
        compiler_params=_params(("arbitrary",)),
        name="peer_v",
    )(idx4.reshape(-1), wgt, _pair_pieces(), v_pk)


SC_CORES = 2
SC_SUBCORES = 16
SC_LANES = 16
SC_WORKERS = SC_CORES * SC_SUBCORES
SC_CHUNK = 32
SC_TOKENS = 28672


def _sc_params():
    cp = pltpu.CompilerParams()
    if "needs_layout_passes" in pltpu.CompilerParams.__dataclass_fields__:
        cp = dataclasses.replace(cp, needs_layout_passes=False)
    return cp


def _peer_v_sc_call(idx, wgt, v3):
    n = idx.shape[0]
    per_worker = n // SC_WORKERS
    groups = LANES // SC_LANES
    mesh = plsc.VectorSubcoreMesh(core_axis_name="c", subcore_axis_name="s")

    @functools.partial(
        pl.kernel, mesh=mesh, compiler_params=_sc_params(),
        out_type=jax.ShapeDtypeStruct((n, SUBLANES, LANES), f32),
        scratch_types=[pltpu.VMEM((PEER_PAIRS,), i32), pltpu.VMEM((PEER_PAIRS,), f32),
                       pltpu.VMEM((SC_CHUNK, SUBLANES, LANES), f32), pltpu.VMEM((SC_CHUNK, SC_LANES), f32),
                       pltpu.VMEM((SUBLANES, LANES), f32), pltpu.SemaphoreType.DMA])
    def sc_kernel(idx_hbm, w_hbm, tab_hbm, out_hbm, idx_v, w_v, rows_v, wb_v, out_v, sem):
        worker = lax.axis_index("s") * SC_CORES + lax.axis_index("c")
        zero = jnp.zeros((SC_LANES,), f32)

        @pl.loop(0, per_worker)
        def _(i):
            t = worker * per_worker + i
            pltpu.sync_copy(idx_hbm.at[t], idx_v)
            pltpu.sync_copy(w_hbm.at[t], w_v)
            for chunk in range(PEER_PAIRS // SC_CHUNK):
                pltpu.async_copy(tab_hbm.at[idx_v.at[pl.ds(chunk * SC_CHUNK, SC_CHUNK)]], rows_v, sem).wait()

                @pl.loop(0, SC_CHUNK)
                def _(r):
                    wb_v[r, :] = plsc.load_gather(w_v, [jnp.full((SC_LANES,), chunk * SC_CHUNK, i32) + r])

                for s2 in range(SUBLANES // 2):
                    def where(a, s2=s2):
                        return 2 * s2 + a // groups, pl.ds((a % groups) * SC_LANES, SC_LANES)

                    def row_body(r, accs):
                        wv = wb_v[r, :]
                        return tuple(accs[a] + wv * rows_v[(r,) + where(a)] for a in range(2 * groups))

                    init = (zero,) * (2 * groups) if chunk == 0 else tuple(out_v[where(a)] for a in range(2 * groups))
                    accs = lax.fori_loop(0, SC_CHUNK, row_body, init)
                    for a in range(2 * groups):
                        out_v[where(a)] = accs[a]
            pltpu.sync_copy(out_v, out_hbm.at[t])

    return sc_kernel(idx, wgt, v3)


def _ln_kernel(x_ref, y_ref, gt_ref, g_ref, b_ref, o_ref):
    tm = x_ref.shape[0]
    y = jnp.concatenate([y_ref[pl.ds(r, tm, stride=SUBLANES), :] for r in range(SUBLANES)], axis=-1)
    o_ref[...] = _layer_norm(DN_ALPHA * x_ref[...] + y * gt_ref[...], g_ref[...], b_ref[...])


def _ln_call(x1, y8, mod6, lng, lnb, S):
    N = x1.shape[0]
    tm = 512
    row = pl.BlockSpec((tm, D_MODEL), lambda i: (i, 0))
    vec = pl.BlockSpec((1, D_MODEL), lambda i: (0, 0))
    return pl.pallas_call(
        _ln_kernel,
        grid=(N // tm,),
        in_specs=[row, pl.BlockSpec((tm * SUBLANES, LANES), lambda i: (i, 0)), _mod_spec(5, S // tm), vec, vec],
        out_specs=row,
        out_shape=jax.ShapeDtypeStruct((N, D_MODEL), f32),
        compiler_params=_params(("arbitrary",)),
        name="ln",
    )(x1, y8, mod6, lng, lnb)


def _pack_table(t):
    tb = lax.bitcast_convert_type(t.astype(bf16), jnp.uint16).astype(jnp.uint32)
    word = tb[:, :ROW_WORDS] | (tb[:, ROW_WORDS:] << 16)
    return lax.bitcast_convert_type(word, i32).reshape(PEER_N * ROW_SUB, LANES)


def _pad_keys(sub_keys_l):
    half = PEER_DKEY // 2
    z = jnp.zeros((PEER_HEADS, PEER_NKEYS, half), f32)
    k1 = jnp.concatenate([sub_keys_l[0], z], axis=-1).astype(bf16)
    k2 = jnp.concatenate([z, sub_keys_l[1]], axis=-1).astype(bf16)
    return k1, k2


def _layer(x, c, lb_logits, B, S, layer, p):
    (w_ada, b_ada, w_in, hgn, wua, wub, wo, wpq, sub_keys, pu, pv, ln_g, ln_b) = p
    mod6 = _mod_call(c, w_ada, b_ada.reshape(1, -1)).reshape(B, 6, 1, D_MODEL)
    proj = _proj_call(x, mod6, w_in.astype(bf16), S)
    oa = _hgrn_call(layer.reshape(1), proj, lb_logits, B, S)
    ob = _sb_call(proj, B, S)
    x1, hf8, pq = _merge_call(x, oa, proj, ob, mod6, hgn.reshape(1, -1), wua.astype(bf16),
                              wub.astype(bf16), wo.astype(bf16), wpq.astype(bf16),
                              ln_g[0:1], ln_b[0:1], S)
    k1p, k2p = _pad_keys(sub_keys)
    idx4, gates = _topk_call(pq, k1p, k2p)
    N = x.shape[0]
    u_pk = _pack_table(pu)
    wgt_a = _peer_u_call(idx4, hf8, gates, u_pk, 0, SC_TOKENS)
    y_a = _peer_v_sc_call(lax.shift_right_logical(idx4[:SC_TOKENS], 2), wgt_a,
                          pv.reshape(PEER_N, SUBLANES, LANES))
    wgt_b = _peer_u_call(idx4, hf8, gates, u_pk, SC_TOKENS, N - SC_TOKENS)
    y_b = _peer_v_call(idx4, wgt_b, _pack_table(pv), SC_TOKENS)
    y8 = jnp.concatenate([y_a.reshape(SC_TOKENS * SUBLANES, LANES), y_b], axis=0)
    return _ln_call(x1, y8, mod6, ln_g[1:2], ln_b[1:2], S)


def kernel(x, c, w_ada, b_ada, w_in, lb_logits, hg_norm_g, w_up_a, w_up_b, w_o, w_pq, sub_keys,
           peer_u, peer_v, ln_g, ln_b):
    B, S, _ = x.shape
    xs = (jnp.arange(DEPTH, dtype=i32), w_ada, b_ada, w_in, hg_norm_g, w_up_a, w_up_b, w_o, w_pq,
          sub_keys, peer_u, peer_v, ln_g, ln_b)

    def step(xc, per_layer):
        return _layer(xc, c, lb_logits, B, S, per_layer[0], per_layer[1:]), None

    out, _ = lax.scan(step, x.reshape(B * S, D_MODEL), xs)
    return out.reshape(B, S, D_MODEL)
```

```python
import dataclasses
import functools
import math

import jax
import jax.numpy as jnp
from jax import lax
from jax.experimental import pallas as pl
from jax.experimental.pallas import tpu as pltpu
from jax.experimental.pallas import tpu_sc as plsc

f32 = jnp.float32
bf16 = jnp.bfloat16
i32 = jnp.int32

D_MODEL = 1024
DEPTH = 4
HG_HEADS = 4
HG_D = 128
HG_W = HG_HEADS * HG_D
SB_HEADS = 8
SB_DH = 64
SB_W = SB_HEADS * SB_DH
SB_BLOCK = 128
PEER_HEADS = 8
PEER_NKEYS = 128
PEER_N = PEER_NKEYS * PEER_NKEYS
PEER_DKEY = 128
PEER_TOPK = 16
PEER_PAIRS = PEER_HEADS * PEER_TOPK
IN_WIDTH = 4 * HG_W + 3 * SB_W + 2 * D_MODEL
DN_ALPHA = (2.0 * DEPTH) ** 0.25
LN_EPS = 1e-5
RMS_EPS = 1e-6

LANES = 128
SUBLANES = 8
ROW_WORDS = D_MODEL // 2
ROW_SUB = ROW_WORDS // LANES
VMEM_LIMIT = 48 * 1024 * 1024

COL_QA, COL_FA, COL_IA = 0, 4, 8
COL_QB, COL_KB, COL_VB = 16, 20, 24
COL512_GA, COL512_GATE_A, COL512_GATE_B = 3, 7, 9


def _nt(a, b):
    return lax.dot_general(a, b, (((1,), (1,)), ((), ())), preferred_element_type=f32)


def _tn(a, b):
    return lax.dot_general(a, b, (((0,), (0,)), ((), ())), preferred_element_type=f32)


def _dot(a, b):
    return jnp.dot(a, b, preferred_element_type=f32)


def _split_bf16(a):
    hi = a.astype(bf16)
    lo = (a - hi.astype(f32)).astype(bf16)
    return hi, lo


def _params(sem):
    return pltpu.CompilerParams(dimension_semantics=sem, vmem_limit_bytes=VMEM_LIMIT)


def _mod_kernel(c_ref, w_ref, b_ref, o_ref):
    c = c_ref[...]
    cond = (c * jax.nn.sigmoid(c)).astype(bf16)
    o_ref[...] = _dot(cond, w_ref[...].astype(bf16)) + b_ref[...]


def _mod_call(c, w_ada_l, b_ada_l):
    B = c.shape[0]
    tn = 1536
    return pl.pallas_call(
        _mod_kernel,
        grid=(6 * D_MODEL // tn,),
        in_specs=[pl.BlockSpec((B, D_MODEL), lambda j: (0, 0)),
                  pl.BlockSpec((D_MODEL, tn), lambda j: (0, j)),
                  pl.BlockSpec((1, tn), lambda j: (0, j))],
        out_specs=pl.BlockSpec((B, tn), lambda j: (0, j)),
        out_shape=jax.ShapeDtypeStruct((B, 6 * D_MODEL), f32),
        compiler_params=_params(("arbitrary",)),
        name="mod",
    )(c, w_ada_l, b_ada_l)


def _mod_spec(which, tiles_per_seq):
    return pl.BlockSpec((None, None, 1, D_MODEL), lambda i: (i // tiles_per_seq, which, 0, 0))


def _proj_kernel(x_ref, sh_ref, sc_ref, w_ref, o_ref):
    hm = (x_ref[...] * (1.0 + sc_ref[...]) + sh_ref[...]).astype(bf16)
    for n0 in range(0, IN_WIDTH, 512):
        o_ref[:, n0:n0 + 512] = _dot(hm, w_ref[:, n0:n0 + 512])


def _proj_call(x, mod6, w_in_bf, S):
    N = x.shape[0]
    tm = 256
    tps = S // tm
    return pl.pallas_call(
        _proj_kernel,
        grid=(N // tm,),
        in_specs=[pl.BlockSpec((tm, D_MODEL), lambda i: (i, 0)),
                  _mod_spec(0, tps), _mod_spec(1, tps),
                  pl.BlockSpec((D_MODEL, IN_WIDTH), lambda i: (0, 0))],
        out_specs=pl.BlockSpec((tm, IN_WIDTH), lambda i: (i, 0)),
        out_shape=jax.ShapeDtypeStruct((N, IN_WIDTH), f32),
        compiler_params=_params(("arbitrary",)),
        name="proj",
    )(x, mod6, mod6, w_in_bf)


HG_CHUNK = 128
HG_SUB = 16
HG_STEP_ROWS = 512


def _hgrn_chunk(q, z, v, lb, st, tri):
    one_m_lb = 1.0 - lb
    g = jnp.log(lb + one_m_lb * jax.nn.sigmoid(z))
    k = one_m_lb * jax.nn.sigmoid(-z)
    g_hi, g_lo = _split_bf16(g)
    b = _dot(tri, g_hi) + _dot(tri, g_lo)
    b_end = b[HG_CHUNK - 1:HG_CHUNK, :]
    inter = _nt((q * jnp.exp(b)).astype(bf16), st.astype(bf16))
    v_bf = v.astype(bf16)
    row_c = lax.broadcasted_iota(i32, (HG_CHUNK, HG_D), 0)
    row_s = lax.broadcasted_iota(i32, (HG_SUB, HG_D), 0)
    blocks = []
    for sub in range(HG_CHUNK // HG_SUB):
        r0 = sub * HG_SUB
        bs = b[r0:r0 + HG_SUB]
        qs = q[r0:r0 + HG_SUB]
        ks = k[r0:r0 + HG_SUB]
        vs = v[r0:r0 + HG_SUB]
        rows = []
        for t in range(HG_SUB):
            m = row_s <= t
            e = jnp.where(m, jnp.exp(jnp.where(m, bs[t:t + 1] - bs, 0.0)), 0.0)
            p = (qs[t:t + 1] * ks) * e
            srow = jnp.sum(p, axis=-1, keepdims=True)
            rows.append(jnp.sum(srow * vs, axis=0, keepdims=True))
        o_sub = jnp.concatenate(rows, axis=0)
        if sub > 0:
            bref = b[r0 - 1:r0, :]
            qi = (qs * jnp.exp(bs - bref)).astype(bf16)
            past = row_c < r0
            ki = jnp.where(past, k * jnp.exp(jnp.where(past, bref - b, 0.0)), 0.0).astype(bf16)
            o_sub = o_sub + _dot(_nt(qi, ki).astype(bf16), v_bf)
        blocks.append(o_sub)
    intra = jnp.concatenate(blocks, axis=0)
    kd = (k * jnp.exp(b_end - b)).astype(bf16)
    st_new = st * jnp.exp(b_end) + _tn(v_bf, kd)
    return inter + intra, st_new


def _hgrn_kernel(l_ref, q_ref, z_ref, v_ref, lbl_ref, o_ref, st_ref):
    @pl.when(pl.program_id(2) == 0)
    def _():
        st_ref[...] = jnp.zeros_like(st_ref)

    logits = lbl_ref[...]
    e = jnp.exp(logits - jnp.max(logits, axis=0, keepdims=True))
    p = e / jnp.sum(e, axis=0, keepdims=True)
    rid = lax.broadcasted_iota(i32, p.shape, 0)
    l = l_ref[0]
    lb = jnp.sum(jnp.where((rid >= 1) & (rid <= l), p, 0.0), axis=0, keepdims=True)

    r = lax.broadcasted_iota(i32, (HG_CHUNK, HG_CHUNK), 0)
    c = lax.broadcasted_iota(i32, (HG_CHUNK, HG_CHUNK), 1)
    tri = jnp.where(c <= r, 1.0, 0.0).astype(bf16)

    def body(ci, st):
        r0 = pl.multiple_of(ci * HG_CHUNK, HG_CHUNK)
        out, st = _hgrn_chunk(q_ref[pl.ds(r0, HG_CHUNK), :], z_ref[pl.ds(r0, HG_CHUNK), :],
                              v_ref[pl.ds(r0, HG_CHUNK), :], lb, st, tri)
        o_ref[pl.ds(r0, HG_CHUNK), :] = out
        return st

    st_ref[...] = lax.fori_loop(0, HG_STEP_ROWS // HG_CHUNK, body, st_ref[...])


def _hgrn_call(layer, proj, lb_logits, B, S):
    N = proj.shape[0]
    R = HG_STEP_ROWS
    spb = S // R

    def col(c0):
        return pl.BlockSpec((R, HG_D), lambda b, h, s, l: (b * spb + s, c0 + h))

    grid_spec = pltpu.PrefetchScalarGridSpec(
        num_scalar_prefetch=1,
        grid=(B, HG_HEADS, spb),
        in_specs=[col(COL_QA), col(COL_FA), col(COL_IA),
                  pl.BlockSpec((DEPTH, HG_D), lambda b, h, s, l: (0, h))],
        out_specs=pl.BlockSpec((R, HG_D), lambda b, h, s, l: (b * spb + s, h)),
        scratch_shapes=[pltpu.VMEM((HG_D, HG_D), f32)],
    )
    return pl.pallas_call(
        _hgrn_kernel,
        grid_spec=grid_spec,
        out_shape=jax.ShapeDtypeStruct((N, HG_W), f32),
        compiler_params=_params(("arbitrary", "arbitrary", "arbitrary")),
        name="hgrn",
    )(layer, proj, proj, proj, lb_logits)


SB_SCALE = 1.0 / math.sqrt(SB_DH)
SB_PAIRS = 4
SB_GROUP = 3
SB_DEAD = -104.0
SB_NEVER = -(1 << 20)


def _sb_kernel(q_ref, k_ref, v_ref, o_ref):
    i = pl.program_id(2)
    T = SB_BLOCK
    P = SB_PAIRS
    lane = lax.broadcasted_iota(i32, (T, T), 1)
    lane2 = lax.broadcasted_iota(i32, (T, 2 * T), 1)
    row2 = lax.broadcasted_iota(i32, (T, 2 * T), 0)
    u = jnp.where((row2 > lane2) | (lane2 >= T), 1.0, 0.0).astype(bf16)
    qms = []
    for p in range(P):
        q2 = q_ref[:, p * T:(p + 1) * T] * SB_SCALE
        qms.append(jnp.concatenate([jnp.where(lane < SB_DH, q2, 0.0), jnp.where(lane >= SB_DH, q2, 0.0)],
                                   axis=0).astype(bf16))
    lane_s = lax.broadcasted_iota(i32, (2 * T, T), 1)
    row_s = lax.broadcasted_iota(i32, (2 * T, T), 0)
    key_minus_query = lane_s - (row_s & (T - 1))

    def trip(state):
        j_hi, carries, accs = state
        carries, accs = list(carries), list(accs)
        chains = [(p, b) for p in range(P) for b in range(SB_GROUP)]
        allowed, ks, vs = [], {}, {}
        for b in range(SB_GROUP):
            j = j_hi - b
            r0 = pl.multiple_of(jnp.maximum(j, 0) * T, T)
            allowed.append(key_minus_query < jnp.where(j >= 0, (i - j) * T, SB_NEVER))
            for p in range(P):
                ks[p, b] = k_ref[pl.ds(r0, T), p * T:(p + 1) * T].astype(bf16)
                vs[p, b] = v_ref[pl.ds(r0, T), p * T:(p + 1) * T].astype(bf16)
        zs = {c: _nt(qms[c[0]], ks[c]) for c in chains}
        lss, his, los = {}, {}, {}
        for c in chains:
            z = zs[c]
            lss[c] = jnp.minimum(z, 0.0) - jnp.log(1.0 + jnp.exp(-jnp.abs(z)))
            his[c], los[c] = _split_bf16(jnp.where(allowed[c[1]], lss[c] - z, 0.0))
        rs = {c: _dot(his[c], u) + _dot(los[c], u) for c in chains}
        ws = {}
        for p, b in chains:
            r = rs[p, b]
            ws[p, b] = jnp.where(allowed[b], jnp.exp(lss[p, b] + r[:, :T] + carries[p]), 0.0).astype(bf16)
            carries[p] = carries[p] + r[:, T:]
        for p, b in chains:
            accs[p] = accs[p] + _dot(ws[p, b], vs[p, b])
        return j_hi - SB_GROUP, tuple(carries), tuple(accs)

    def live(state):
        j_hi, carries, _ = state
        top = carries[0]
        for c in carries[1:]:
            top = jnp.maximum(top, c)
        return (j_hi >= 0) & (jnp.max(top) > SB_DEAD)

    zero = jnp.zeros((2 * T, T), f32)
    _, _, accs = lax.while_loop(live, trip, (i, (zero,) * P, (zero,) * P))
    for p in range(P):
        o_ref[:, p * T:(p + 1) * T] = jnp.where(lane < SB_DH, accs[p][:T], accs[p][T:])


def _sb_call(proj, B, S):
    N = proj.shape[0]
    nq = S // SB_BLOCK
    W = SB_PAIRS * LANES
    return pl.pallas_call(
        _sb_kernel,
        grid=(B, SB_HEADS // (2 * SB_PAIRS), nq),
        in_specs=[pl.BlockSpec((SB_BLOCK, W), lambda b, p, i: (b * nq + i, COL_QB // SB_PAIRS + p)),
                  pl.BlockSpec((S, W), lambda b, p, i: (b, COL_KB // SB_PAIRS + p)),
                  pl.BlockSpec((S, W), lambda b, p, i: (b, COL_VB // SB_PAIRS + p))],
        out_specs=pl.BlockSpec((SB_BLOCK, W), lambda b, p, i: (b * nq + i, p)),
        out_shape=jax.ShapeDtypeStruct((N, SB_W), f32),
        compiler_params=_params(("arbitrary", "arbitrary", "arbitrary")),
        name="sb",
    )(proj, proj, proj)


def _layer_norm(r, g, b):
    mu = jnp.mean(r, axis=-1, keepdims=True)
    d = r - mu
    var = jnp.mean(d * d, axis=-1, keepdims=True)
    return d * lax.rsqrt(var + LN_EPS) * g + b


def _merge_kernel(x_ref, oa_ref, ga_ref, ob_ref, gta0_ref, gta1_ref, gtb0_ref, gtb1_ref,
                  gt1_ref, sh2_ref, sc2_ref, hgn_ref, wua_ref, wub_ref, wo_ref, wpq_ref,
                  lng_ref, lnb_ref, x1_ref, hf_ref, pq_ref):
    oa = oa_ref[...]
    hgn = hgn_ref[...]
    segs = []
    for h in range(HG_HEADS):
        seg = oa[:, h * HG_D:(h + 1) * HG_D]
        ms = jnp.mean(seg * seg, axis=-1, keepdims=True)
        segs.append(seg * lax.rsqrt(ms + RMS_EPS) * hgn[:, h * HG_D:(h + 1) * HG_D])
    ga = ga_ref[...]
    oa_n = jnp.concatenate(segs, axis=-1) * (ga * jax.nn.sigmoid(ga))
    ma = _dot(oa_n.astype(bf16), wua_ref[...])
    mb = _dot(ob_ref[...].astype(bf16), wub_ref[...])
    gate_a = jnp.concatenate([gta0_ref[...], gta1_ref[...]], axis=-1)
    gate_b = jnp.concatenate([gtb0_ref[...], gtb1_ref[...]], axis=-1)
    merged = jax.nn.sigmoid(gate_a) * ma + jax.nn.sigmoid(gate_b) * mb
    y = _dot(merged.astype(bf16), wo_ref[...]) * gt1_ref[...]
    x1 = _layer_norm(DN_ALPHA * x_ref[...] + y, lng_ref[...], lnb_ref[...])
    x1_ref[...] = x1
    hf = x1 * (1.0 + sc2_ref[...]) + sh2_ref[...]
    tm = hf.shape[0]
    for r in range(SUBLANES):
        hf_ref[pl.ds(r, tm, stride=SUBLANES), :] = hf[:, r * LANES:(r + 1) * LANES]
    pq_ref[...] = _dot(hf.astype(bf16), wpq_ref[...])


def _merge_call(x, oa, proj, ob, mod6, hgn, wua, wub, wo, wpq, lng, lnb, S):
    N = x.shape[0]
    tm = 256
    tps = S // tm

    def full(shape):
        return pl.BlockSpec(shape, lambda i: (0,) * len(shape))

    def p512(c):
        return pl.BlockSpec((tm, 512), lambda i: (i, c))

    row = pl.BlockSpec((tm, D_MODEL), lambda i: (i, 0))
    out = jax.ShapeDtypeStruct((N, D_MODEL), f32)
    return pl.pallas_call(
        _merge_kernel,
        grid=(N // tm,),
        in_specs=[row, pl.BlockSpec((tm, HG_W), lambda i: (i, 0)), p512(COL512_GA),
                  pl.BlockSpec((tm, SB_W), lambda i: (i, 0)),
                  p512(COL512_GATE_A), p512(COL512_GATE_A + 1),
                  p512(COL512_GATE_B), p512(COL512_GATE_B + 1),
                  _mod_spec(2, tps), _mod_spec(3, tps), _mod_spec(4, tps),
                  full((1, HG_W)), full((HG_W, D_MODEL)), full((SB_W, D_MODEL)),
                  full((D_MODEL, D_MODEL)), full((D_MODEL, D_MODEL)),
                  full((1, D_MODEL)), full((1, D_MODEL))],
        out_specs=[row, pl.BlockSpec((tm * SUBLANES, LANES), lambda i: (i, 0)), row],
        out_shape=[out, jax.ShapeDtypeStruct((N * SUBLANES, LANES), f32), out],
        compiler_params=_params(("arbitrary",)),
        name="merge",
    )(x, oa, proj, ob, proj, proj, proj, proj, mod6, mod6, mod6, hgn, wua, wub, wo, wpq, lng, lnb)


TOPK_TOKENS = 256


def _top16(s, payload=None):
    R = s.shape[0]
    rid = lax.broadcasted_iota(i32, s.shape, 0).astype(f32)
    vals, ids = [], []
    for _ in range(PEER_TOPK):
        m = jnp.max(s, axis=0, keepdims=True)
        first = jnp.min(jnp.where(s == m, rid, float(R)), axis=0, keepdims=True)
        sel = rid == first
        vals.append(m)
        if payload is None:
            ids.append(first.astype(i32))
        else:
            ids.append(jnp.max(jnp.where(sel, payload, -1), axis=0, keepdims=True))
        s = jnp.where(sel, -jnp.inf, s)
    return vals, ids


def _topk_kernel(pq_ref, k1_ref, k2_ref, idx_ref, gate_ref, idx_t, gate_t):
    H = SUBLANES

    def head(h, _):
        c0 = pl.multiple_of(h * PEER_DKEY, PEER_DKEY)
        qh = pq_ref[:, pl.ds(c0, PEER_DKEY)].astype(bf16)
        s1 = _nt(k1_ref[h], qh)
        s2 = _nt(k2_ref[h], qh)
        v1, i1 = _top16(s1)
        v2, i2 = _top16(s2)
        v1a, i1a = jnp.concatenate(v1, axis=0), jnp.concatenate(i1, axis=0)
        v2a, i2a = jnp.concatenate(v2, axis=0), jnp.concatenate(i2, axis=0)
        cand = ([v1[0] + v2a[:H], v1[0] + v2a[H:]] + [v1[a] + v2a[:H] for a in range(1, H)]
                + [v1a[H:] + v2[0]])
        cidx = ([i1[0] * PEER_NKEYS + i2a[:H], i1[0] * PEER_NKEYS + i2a[H:]]
                + [i1[a] * PEER_NKEYS + i2a[:H] for a in range(1, H)] + [i1a[H:] * PEER_NKEYS + i2[0]])
        tv, ti = _top16(jnp.concatenate(cand, axis=0), jnp.concatenate(cidx, axis=0) * ROW_SUB)
        tva = jnp.concatenate(tv, axis=0)
        e = jnp.exp(tva - tv[0])
        r0 = pl.multiple_of(h * PEER_TOPK, PEER_TOPK)
        gate_t[pl.ds(r0, PEER_TOPK), :] = e / jnp.sum(e, axis=0, keepdims=True)
        idx_t[pl.ds(r0, PEER_TOPK), :] = jnp.concatenate(ti, axis=0)
        return 0

    lax.fori_loop(0, PEER_HEADS, head, 0)
    idx_ref[...] = idx_t[...].T
    gate_ref[...] = gate_t[...].T


def _topk_call(pq, k1p, k2p):
    N = pq.shape[0]
    T = TOPK_TOKENS
    keys = pl.BlockSpec((PEER_HEADS, PEER_NKEYS, PEER_DKEY), lambda i: (0, 0, 0))
    return pl.pallas_call(
        _topk_kernel,
        grid=(N // T,),
        in_specs=[pl.BlockSpec((T, D_MODEL), lambda i: (i, 0)), keys, keys],
        out_specs=[pl.BlockSpec((T, PEER_PAIRS), lambda i: (i, 0)),
                   pl.BlockSpec((T, PEER_PAIRS), lambda i: (i, 0))],
        out_shape=[jax.ShapeDtypeStruct((N, PEER_PAIRS), i32),
                   jax.ShapeDtypeStruct((N, PEER_PAIRS), f32)],
        scratch_shapes=[pltpu.VMEM((PEER_PAIRS, T), i32), pltpu.VMEM((PEER_PAIRS, T), f32)],
        compiler_params=_params(("arbitrary",)),
        name="topk",
    )(pq, k1p, k2p)


PEER_TOKENS = 64
PEER_GROUP = SUBLANES
STAGE_ROWS = PEER_PAIRS * ROW_SUB
STAGE_COLS = 2 * STAGE_ROWS
INV_SQRT2 = 1.0 / math.sqrt(2.0)


def _gather_rows(idx_s, k0, tab_ref, stage_ref):
    for k in range(PEER_PAIRS):
        e4 = pl.multiple_of(idx_s[k0 + k], ROW_SUB)
        stage_ref[k * ROW_SUB:(k + 1) * ROW_SUB, :] = tab_ref[pl.ds(e4, ROW_SUB), :]


def _piece_diag():
    sub = lax.broadcasted_iota(i32, (SUBLANES, STAGE_COLS), 0)
    col = lax.broadcasted_iota(i32, (SUBLANES, STAGE_COLS), 1)
    return (col & (SUBLANES - 1)) == 2 * (sub & (ROW_SUB - 1)) + lax.shift_right_logical(sub, 2)


def _peer_u_kernel(idx_s, hf_ref, g_ref, fold_ref, tab_ref, o_ref, stage, zs_ref):
    T = o_ref.shape[0]
    G = PEER_GROUP
    diag = _piece_diag()

    def group(gi, _):
        g0 = pl.multiple_of(gi * G, G)
        for t in range(G):
            _gather_rows(idx_s, (g0 + t) * PEER_PAIRS, tab_ref, stage)
            h_hi, h_lo = _split_bf16(hf_ref[pl.ds(pl.multiple_of((g0 + t) * SUBLANES, SUBLANES), SUBLANES), :])
            z = _nt(jnp.concatenate([h_hi, h_lo], axis=0), pltpu.bitcast(stage[...], bf16))
            zs_ref[t:t + 1, :] = jnp.sum(jnp.where(diag, z[:SUBLANES] + z[SUBLANES:], 0.0), axis=0, keepdims=True)
        zs_hi, zs_lo = _split_bf16(zs_ref[...])
        dots = _dot(zs_hi, fold_ref[...]) + _dot(zs_lo, fold_ref[...])
        act = 0.5 * dots * (1.0 + lax.erf(dots * INV_SQRT2))
        o_ref[pl.ds(g0, G), :] = act * g_ref[pl.ds(g0, G), :]
        return 0

    lax.fori_loop(0, T // G, group, 0)


def _peer_v_kernel(idx_s, w_ref, spread_ref, tab_ref, o_ref, stage):
    T = w_ref.shape[0]
    G = PEER_GROUP
    diag = _piece_diag()

    def group(gi, _):
        g0 = pl.multiple_of(gi * G, G)
        w8 = _dot(w_ref[pl.ds(g0, G), :].astype(bf16), spread_ref[...])
        for t in range(G):
            _gather_rows(idx_s, (g0 + t) * PEER_PAIRS, tab_ref, stage)
            wexp = jnp.where(diag, jnp.broadcast_to(w8[t:t + 1, :], diag.shape), 0.0).astype(bf16)
            r0 = pl.multiple_of((g0 + t) * SUBLANES, SUBLANES)
            o_ref[pl.ds(r0, SUBLANES), :] = _dot(wexp, pltpu.bitcast(stage[...], bf16))
        return 0

    lax.fori_loop(0, T // G, group, 0)


def _table_spec():
    return pl.BlockSpec((PEER_N * ROW_SUB, LANES), lambda i: (0, 0), pipeline_mode=pl.Buffered(1))


def _smem_flat(T, first):
    return pl.BlockSpec((T * PEER_PAIRS,), lambda i: (i + first,), memory_space=pltpu.SMEM)


def _pair_pieces():
    k = lax.broadcasted_iota(i32, (PEER_PAIRS, STAGE_COLS), 0)
    c = lax.broadcasted_iota(i32, (PEER_PAIRS, STAGE_COLS), 1)
    return (c // SUBLANES == k).astype(bf16)


def _peer_u_call(idx4, hf8, gates, u_pk, start, count):
    T = PEER_TOKENS
    first = start // T
    return pl.pallas_call(
        _peer_u_kernel,
        grid=(count // T,),
        in_specs=[_smem_flat(T, first),
                  pl.BlockSpec((T * SUBLANES, LANES), lambda i: (i + first, 0)),
                  pl.BlockSpec((T, PEER_PAIRS), lambda i: (i + first, 0)),
                  pl.BlockSpec((STAGE_COLS, PEER_PAIRS), lambda i: (0, 0)), _table_spec()],
        out_specs=pl.BlockSpec((T, PEER_PAIRS), lambda i: (i, 0)),
        out_shape=jax.ShapeDtypeStruct((count, PEER_PAIRS), f32),
        scratch_shapes=[pltpu.VMEM((STAGE_ROWS, LANES), i32), pltpu.VMEM((PEER_GROUP, STAGE_COLS), f32)],
        compiler_params=_params(("arbitrary",)),
        name="peer_u",
    )(idx4.reshape(-1), hf8, gates, _pair_pieces().T, u_pk)


def _peer_v_call(idx4, wgt, v_pk, start):
    count = wgt.shape[0]
    T = PEER_TOKENS
    return pl.pallas_call(
        _peer_v_kernel,
        grid=(count // T,),
        in_specs=[_smem_flat(T, start // T), pl.BlockSpec((T, PEER_PAIRS), lambda i: (i, 0)),
                  pl.BlockSpec((PEER_PAIRS, STAGE_COLS), lambda i: (0, 0)), _table_spec()],
        out_specs=pl.BlockSpec((T * SUBLANES, LANES), lambda i: (i, 0)),
        out_shape=jax.ShapeDtypeStruct((count * SUBLANES, LANES), f32),
        scratch_shapes=[pltpu.VMEM((STAGE_ROWS, LANES), i32)],
        compiler_params=_params(("arbitrary",)),
        name="peer_v",
    )(idx4.reshape(-1), wgt, _pair_pieces(), v_pk)


SC_CORES = 2
SC_SUBCORES = 16
SC_LANES = 16
SC_WORKERS = SC_CORES * SC_SUBCORES
SC_CHUNK = 16
SC_GROUP = 8
SC_TOKENS = 32768


def _sc_params():
    cp = pltpu.CompilerParams()
    if "needs_layout_passes" in pltpu.CompilerParams.__dataclass_fields__:
        cp = dataclasses.replace(cp, needs_layout_passes=False)
    return cp


def _peer_v_sc_call(idx, wgt, v3):
    n = idx.shape[0]
    per_worker = n // SC_WORKERS
    groups = LANES // SC_LANES
    n_chunks = PEER_PAIRS // SC_CHUNK
    mesh = plsc.VectorSubcoreMesh(core_axis_name="c", subcore_axis_name="s")
    rows_t = pltpu.VMEM((SC_CHUNK, SUBLANES, LANES), f32)

    @functools.partial(
        pl.kernel, mesh=mesh, compiler_params=_sc_params(),
        out_type=jax.ShapeDtypeStruct((n, SUBLANES, LANES), f32),
        scratch_types=[pltpu.VMEM((SC_GROUP, PEER_PAIRS), i32), pltpu.VMEM((SC_GROUP, PEER_PAIRS), f32),
                       rows_t, rows_t, pltpu.VMEM((SC_CHUNK, SC_LANES), f32),
                       pltpu.VMEM((SC_GROUP, SUBLANES, LANES), f32),
                       pltpu.SemaphoreType.DMA, pltpu.SemaphoreType.DMA])
    def sc_kernel(idx_hbm, w_hbm, tab_hbm, out_hbm, idx_v, w_v, rows_a, rows_b, wb_v, out_v, sem_a, sem_b):
        worker = lax.axis_index("s") * SC_CORES + lax.axis_index("c")
        zero = jnp.zeros((SC_LANES,), f32)
        bufs = ((rows_a, sem_a), (rows_b, sem_b))

        def gather(j, c):
            rows, sem = bufs[c % 2]
            return pltpu.make_async_copy(tab_hbm.at[idx_v.at[j, pl.ds(c * SC_CHUNK, SC_CHUNK)]], rows, sem)

        @pl.loop(0, per_worker // SC_GROUP)
        def _(g):
            t0 = worker * per_worker + g * SC_GROUP
            pltpu.sync_copy(idx_hbm.at[pl.ds(t0, SC_GROUP)], idx_v)
            pltpu.sync_copy(w_hbm.at[pl.ds(t0, SC_GROUP)], w_v)
            gather(0, 0).start()

            @pl.loop(0, SC_GROUP)
            def _(j):
                for c in range(n_chunks):
                    if c + 1 < n_chunks:
                        gather(j, c + 1).start()
                    else:
                        @pl.when(j + 1 < SC_GROUP)
                        def _():
                            gather(j + 1, 0).start()
                    gather(j, c).wait()
                    rows = bufs[c % 2][0]

                    @pl.loop(0, SC_CHUNK)
                    def _(r):
                        wb_v[r, :] = plsc.load_gather(
                            w_v, [jnp.full((SC_LANES,), 0, i32) + j, jnp.full((SC_LANES,), c * SC_CHUNK, i32) + r])

                    for s2 in range(SUBLANES // 2):
                        def where(a, s2=s2):
                            return 2 * s2 + a // groups, pl.ds((a % groups) * SC_LANES, SC_LANES)

                        def row_body(r, accs, rows=rows, where=where):
                            wv = wb_v[r, :]
                            return tuple(accs[a] + wv * rows[(r,) + where(a)] for a in range(2 * groups))

                        if c == 0:
                            init = (zero,) * (2 * groups)
                        else:
                            init = tuple(out_v[(j,) + where(a)] for a in range(2 * groups))
                        accs = lax.fori_loop(0, SC_CHUNK, row_body, init)
                        for a in range(2 * groups):
                            out_v[(j,) + where(a)] = accs[a]

            pltpu.sync_copy(out_v, out_hbm.at[pl.ds(t0, SC_GROUP)])

    return sc_kernel(idx, wgt, v3)


def _ln_kernel(x_ref, y_ref, gt_ref, g_ref, b_ref, o_ref):
    tm = x_ref.shape[0]
    y = jnp.concatenate([y_ref[pl.ds(r, tm, stride=SUBLANES), :] for r in range(SUBLANES)], axis=-1)
    o_ref[...] = _layer_norm(DN_ALPHA * x_ref[...] + y * gt_ref[...], g_ref[...], b_ref[...])


def _ln_call(x1, y8, mod6, lng, lnb, S):
    N = x1.shape[0]
    tm = 512
    row = pl.BlockSpec((tm, D_MODEL), lambda i: (i, 0))
    vec = pl.BlockSpec((1, D_MODEL), lambda i: (0, 0))
    return pl.pallas_call(
        _ln_kernel,
        grid=(N // tm,),
        in_specs=[row, pl.BlockSpec((tm * SUBLANES, LANES), lambda i: (i, 0)), _mod_spec(5, S // tm), vec, vec],
        out_specs=row,
        out_shape=jax.ShapeDtypeStruct((N, D_MODEL), f32),
        compiler_params=_params(("arbitrary",)),
        name="ln",
    )(x1, y8, mod6, lng, lnb)


def _pack_table(t):
    tb = lax.bitcast_convert_type(t.astype(bf16), jnp.uint16).astype(jnp.uint32)
    word = tb[:, :ROW_WORDS] | (tb[:, ROW_WORDS:] << 16)
    return lax.bitcast_convert_type(word, i32).reshape(PEER_N * ROW_SUB, LANES)


def _pad_keys(sub_keys_l):
    half = PEER_DKEY // 2
    z = jnp.zeros((PEER_HEADS, PEER_NKEYS, half), f32)
    k1 = jnp.concatenate([sub_keys_l[0], z], axis=-1).astype(bf16)
    k2 = jnp.concatenate([z, sub_keys_l[1]], axis=-1).astype(bf16)
    return k1, k2


def _layer(x, c, lb_logits, B, S, layer, p):
    (w_ada, b_ada, w_in, hgn, wua, wub, wo, wpq, sub_keys, pu, pv, ln_g, ln_b) = p
    mod6 = _mod_call(c, w_ada, b_ada.reshape(1, -1)).reshape(B, 6, 1, D_MODEL)
    proj = _proj_call(x, mod6, w_in.astype(bf16), S)
    oa = _hgrn_call(layer.reshape(1), proj, lb_logits, B, S)
    ob = _sb_call(proj, B, S)
    x1, hf8, pq = _merge_call(x, oa, proj, ob, mod6, hgn.reshape(1, -1), wua.astype(bf16),
                              wub.astype(bf16), wo.astype(bf16), wpq.astype(bf16),
                              ln_g[0:1], ln_b[0:1], S)
    k1p, k2p = _pad_keys(sub_keys)
    idx4, gates = _topk_call(pq, k1p, k2p)
    N = x.shape[0]
    u_pk = _pack_table(pu)
    wgt_a = _peer_u_call(idx4, hf8, gates, u_pk, 0, SC_TOKENS)
    y_a = _peer_v_sc_call(lax.shift_right_logical(idx4[:SC_TOKENS], 2), wgt_a,
                          pv.reshape(PEER_N, SUBLANES, LANES))
    wgt_b = _peer_u_call(idx4, hf8, gates, u_pk, SC_TOKENS, N - SC_TOKENS)
    y_b = _peer_v_call(idx4, wgt_b, _pack_table(pv), SC_TOKENS)
    y8 = jnp.concatenate([y_a.reshape(SC_TOKENS * SUBLANES, LANES), y_b], axis=0)
    return _ln_call(x1, y8, mod6, ln_g[1:2], ln_b[1:2], S)


def kernel(x, c, w_ada, b_ada, w_in, lb_logits, hg_norm_g, w_up_a, w_up_b, w_o, w_pq, sub_keys,
           peer_u, peer_v, ln_g, ln_b):
    B, S, _ = x.shape
    xs = (jnp.arange(DEPTH, dtype=i32), w_ada, b_ada, w_in, hg_norm_g, w_up_a, w_up_b, w_o, w_pq,
          sub_keys, peer_u, peer_v, ln_g, ln_b)

    def step(xc, per_layer):
        return _layer(xc, c, lb_logits, B, S, per_layer[0], per_layer[1:]), None

    out, _ = lax.scan(step, x.reshape(B * S, D_MODEL), xs)
    return out.reshape(B, S, D_MODEL)
```

```python
import dataclasses
import functools
import math

import jax
import jax.numpy as jnp
from jax import lax
from jax.experimental import pallas as pl
from jax.experimental.pallas import tpu as pltpu
from jax.experimental.pallas import tpu_sc as plsc

f32 = jnp.float32
bf16 = jnp.bfloat16
i32 = jnp.int32

D_MODEL = 1024
DEPTH = 4
HG_HEADS = 4
HG_D = 128
HG_W = HG_HEADS * HG_D
SB_HEADS = 8
SB_DH = 64
SB_W = SB_HEADS * SB_DH
SB_BLOCK = 128
PEER_HEADS = 8
PEER_NKEYS = 128
PEER_N = PEER_NKEYS * PEER_NKEYS
PEER_DKEY = 128
PEER_TOPK = 16
PEER_PAIRS = PEER_HEADS * PEER_TOPK
IN_WIDTH = 4 * HG_W + 3 * SB_W + 2 * D_MODEL
DN_ALPHA = (2.0 * DEPTH) ** 0.25
LN_EPS = 1e-5
RMS_EPS = 1e-6

LANES = 128
SUBLANES = 8
ROW_WORDS = D_MODEL // 2
ROW_SUB = ROW_WORDS // LANES
VMEM_LIMIT = 48 * 1024 * 1024

COL_QA, COL_FA, COL_IA = 0, 4, 8
COL_QB, COL_KB, COL_VB = 16, 20, 24
COL512_GA, COL512_GATE_A, COL512_GATE_B = 3, 7, 9


def _nt(a, b):
    return lax.dot_general(a, b, (((1,), (1,)), ((), ())), preferred_element_type=f32)


def _tn(a, b):
    return lax.dot_general(a, b, (((0,), (0,)), ((), ())), preferred_element_type=f32)


def _dot(a, b):
    return jnp.dot(a, b, preferred_element_type=f32)


def _split_bf16(a):
    hi = a.astype(bf16)
    lo = (a - hi.astype(f32)).astype(bf16)
    return hi, lo


def _params(sem):
    return pltpu.CompilerParams(dimension_semantics=sem, vmem_limit_bytes=VMEM_LIMIT)


def _mod_kernel(c_ref, w_ref, b_ref, o_ref):
    c = c_ref[...]
    cond = (c * jax.nn.sigmoid(c)).astype(bf16)
    o_ref[...] = _dot(cond, w_ref[...].astype(bf16)) + b_ref[...]


def _mod_call(c, w_ada_l, b_ada_l):
    B = c.shape[0]
    tn = 1536
    return pl.pallas_call(
        _mod_kernel,
        grid=(6 * D_MODEL // tn,),
        in_specs=[pl.BlockSpec((B, D_MODEL), lambda j: (0, 0)),
                  pl.BlockSpec((D_MODEL, tn), lambda j: (0, j)),
                  pl.BlockSpec((1, tn), lambda j: (0, j))],
        out_specs=pl.BlockSpec((B, tn), lambda j: (0, j)),
        out_shape=jax.ShapeDtypeStruct((B, 6 * D_MODEL), f32),
        compiler_params=_params(("arbitrary",)),
        name="mod",
    )(c, w_ada_l, b_ada_l)


def _mod_spec(which, tiles_per_seq):
    return pl.BlockSpec((None, None, 1, D_MODEL), lambda i: (i // tiles_per_seq, which, 0, 0))


def _proj_kernel(x_ref, sh_ref, sc_ref, w_ref, o_ref):
    hm = (x_ref[...] * (1.0 + sc_ref[...]) + sh_ref[...]).astype(bf16)
    for n0 in range(0, IN_WIDTH, 512):
        o_ref[:, n0:n0 + 512] = _dot(hm, w_ref[:, n0:n0 + 512])


def _proj_call(x, mod6, w_in_bf, S):
    N = x.shape[0]
    tm = 256
    tps = S // tm
    return pl.pallas_call(
        _proj_kernel,
        grid=(N // tm,),
        in_specs=[pl.BlockSpec((tm, D_MODEL), lambda i: (i, 0)),
                  _mod_spec(0, tps), _mod_spec(1, tps),
                  pl.BlockSpec((D_MODEL, IN_WIDTH), lambda i: (0, 0))],
        out_specs=pl.BlockSpec((tm, IN_WIDTH), lambda i: (i, 0)),
        out_shape=jax.ShapeDtypeStruct((N, IN_WIDTH), f32),
        compiler_params=_params(("arbitrary",)),
        name="proj",
    )(x, mod6, mod6, w_in_bf)


HG_CHUNK = 128
HG_SUB = 16
HG_STEP_ROWS = 512


def _hgrn_chunk(q, z, v, lb, st, tri):
    one_m_lb = 1.0 - lb
    g = jnp.log(lb + one_m_lb * jax.nn.sigmoid(z))
    k = one_m_lb * jax.nn.sigmoid(-z)
    g_hi, g_lo = _split_bf16(g)
    b = _dot(tri, g_hi) + _dot(tri, g_lo)
    b_end = b[HG_CHUNK - 1:HG_CHUNK, :]
    inter = _nt((q * jnp.exp(b)).astype(bf16), st.astype(bf16))
    v_bf = v.astype(bf16)
    row_c = lax.broadcasted_iota(i32, (HG_CHUNK, HG_D), 0)
    row_s = lax.broadcasted_iota(i32, (HG_SUB, HG_D), 0)
    blocks = []
    for sub in range(HG_CHUNK // HG_SUB):
        r0 = sub * HG_SUB
        bs = b[r0:r0 + HG_SUB]
        qs = q[r0:r0 + HG_SUB]
        ks = k[r0:r0 + HG_SUB]
        vs = v[r0:r0 + HG_SUB]
        rows = []
        for t in range(HG_SUB):
            m = row_s <= t
            e = jnp.where(m, jnp.exp(jnp.where(m, bs[t:t + 1] - bs, 0.0)), 0.0)
            p = (qs[t:t + 1] * ks) * e
            srow = jnp.sum(p, axis=-1, keepdims=True)
            rows.append(jnp.sum(srow * vs, axis=0, keepdims=True))
        o_sub = jnp.concatenate(rows, axis=0)
        if sub > 0:
            bref = b[r0 - 1:r0, :]
            qi = (qs * jnp.exp(bs - bref)).astype(bf16)
            past = row_c < r0
            ki = jnp.where(past, k * jnp.exp(jnp.where(past, bref - b, 0.0)), 0.0).astype(bf16)
            o_sub = o_sub + _dot(_nt(qi, ki).astype(bf16), v_bf)
        blocks.append(o_sub)
    intra = jnp.concatenate(blocks, axis=0)
    kd = (k * jnp.exp(b_end - b)).astype(bf16)
    st_new = st * jnp.exp(b_end) + _tn(v_bf, kd)
    return inter + intra, st_new


def _hgrn_kernel(l_ref, q_ref, z_ref, v_ref, lbl_ref, o_ref, st_ref):
    @pl.when(pl.program_id(2) == 0)
    def _():
        st_ref[...] = jnp.zeros_like(st_ref)

    logits = lbl_ref[...]
    e = jnp.exp(logits - jnp.max(logits, axis=0, keepdims=True))
    p = e / jnp.sum(e, axis=0, keepdims=True)
    rid = lax.broadcasted_iota(i32, p.shape, 0)
    l = l_ref[0]
    lb = jnp.sum(jnp.where((rid >= 1) & (rid <= l), p, 0.0), axis=0, keepdims=True)

    r = lax.broadcasted_iota(i32, (HG_CHUNK, HG_CHUNK), 0)
    c = lax.broadcasted_iota(i32, (HG_CHUNK, HG_CHUNK), 1)
    tri = jnp.where(c <= r, 1.0, 0.0).astype(bf16)

    def body(ci, st):
        r0 = pl.multiple_of(ci * HG_CHUNK, HG_CHUNK)
        out, st = _hgrn_chunk(q_ref[pl.ds(r0, HG_CHUNK), :], z_ref[pl.ds(r0, HG_CHUNK), :],
                              v_ref[pl.ds(r0, HG_CHUNK), :], lb, st, tri)
        o_ref[pl.ds(r0, HG_CHUNK), :] = out
        return st

    st_ref[...] = lax.fori_loop(0, HG_STEP_ROWS // HG_CHUNK, body, st_ref[...])


def _hgrn_call(layer, proj, lb_logits, B, S):
    N = proj.shape[0]
    R = HG_STEP_ROWS
    spb = S // R

    def col(c0):
        return pl.BlockSpec((R, HG_D), lambda b, h, s, l: (b * spb + s, c0 + h))

    grid_spec = pltpu.PrefetchScalarGridSpec(
        num_scalar_prefetch=1,
        grid=(B, HG_HEADS, spb),
        in_specs=[col(COL_QA), col(COL_FA), col(COL_IA),
                  pl.BlockSpec((DEPTH, HG_D), lambda b, h, s, l: (0, h))],
        out_specs=pl.BlockSpec((R, HG_D), lambda b, h, s, l: (b * spb + s, h)),
        scratch_shapes=[pltpu.VMEM((HG_D, HG_D), f32)],
    )
    return pl.pallas_call(
        _hgrn_kernel,
        grid_spec=grid_spec,
        out_shape=jax.ShapeDtypeStruct((N, HG_W), f32),
        compiler_params=_params(("arbitrary", "arbitrary", "arbitrary")),
        name="hgrn",
    )(layer, proj, proj, proj, lb_logits)


SB_SCALE = 1.0 / math.sqrt(SB_DH)
SB_PAIRS = 4
SB_GROUP = 3
SB_DEAD = -104.0
SB_NEVER = -(1 << 20)


def _sb_kernel(q_ref, k_ref, v_ref, o_ref):
    i = pl.program_id(2)
    T = SB_BLOCK
    P = SB_PAIRS
    lane = lax.broadcasted_iota(i32, (T, T), 1)
    lane2 = lax.broadcasted_iota(i32, (T, 2 * T), 1)
    row2 = lax.broadcasted_iota(i32, (T, 2 * T), 0)
    u = jnp.where((row2 > lane2) | (lane2 >= T), 1.0, 0.0).astype(bf16)
    qms = []
    for p in range(P):
        q2 = q_ref[:, p * T:(p + 1) * T] * SB_SCALE
        qms.append(jnp.concatenate([jnp.where(lane < SB_DH, q2, 0.0), jnp.where(lane >= SB_DH, q2, 0.0)],
                                   axis=0).astype(bf16))
    lane_s = lax.broadcasted_iota(i32, (2 * T, T), 1)
    row_s = lax.broadcasted_iota(i32, (2 * T, T), 0)
    key_minus_query = lane_s - (row_s & (T - 1))

    def trip(state):
        j_hi, carries, accs = state
        carries, accs = list(carries), list(accs)
        chains = [(p, b) for p in range(P) for b in range(SB_GROUP)]
        allowed, ks, vs = [], {}, {}
        for b in range(SB_GROUP):
            j = j_hi - b
            r0 = pl.multiple_of(jnp.maximum(j, 0) * T, T)
            allowed.append(key_minus_query < jnp.where(j >= 0, (i - j) * T, SB_NEVER))
            for p in range(P):
                ks[p, b] = k_ref[pl.ds(r0, T), p * T:(p + 1) * T].astype(bf16)
                vs[p, b] = v_ref[pl.ds(r0, T), p * T:(p + 1) * T].astype(bf16)
        zs = {c: _nt(qms[c[0]], ks[c]) for c in chains}
        lss, his, los = {}, {}, {}
        for c in chains:
            z = zs[c]
            lss[c] = jnp.minimum(z, 0.0) - jnp.log(1.0 + jnp.exp(-jnp.abs(z)))
            his[c], los[c] = _split_bf16(jnp.where(allowed[c[1]], lss[c] - z, 0.0))
        rs = {c: _dot(his[c], u) + _dot(los[c], u) for c in chains}
        ws = {}
        for p, b in chains:
            r = rs[p, b]
            ws[p, b] = jnp.where(allowed[b], jnp.exp(lss[p, b] + r[:, :T] + carries[p]), 0.0).astype(bf16)
            carries[p] = carries[p] + r[:, T:]
        for p, b in chains:
            accs[p] = accs[p] + _dot(ws[p, b], vs[p, b])
        return j_hi - SB_GROUP, tuple(carries), tuple(accs)

    def live(state):
        j_hi, carries, _ = state
        top = carries[0]
        for c in carries[1:]:
            top = jnp.maximum(top, c)
        return (j_hi >= 0) & (jnp.max(top) > SB_DEAD)

    zero = jnp.zeros((2 * T, T), f32)
    _, _, accs = lax.while_loop(live, trip, (i, (zero,) * P, (zero,) * P))
    for p in range(P):
        o_ref[:, p * T:(p + 1) * T] = jnp.where(lane < SB_DH, accs[p][:T], accs[p][T:])


def _sb_call(proj, B, S):
    N = proj.shape[0]
    nq = S // SB_BLOCK
    W = SB_PAIRS * LANES
    return pl.pallas_call(
        _sb_kernel,
        grid=(B, SB_HEADS // (2 * SB_PAIRS), nq),
        in_specs=[pl.BlockSpec((SB_BLOCK, W), lambda b, p, i: (b * nq + i, COL_QB // SB_PAIRS + p)),
                  pl.BlockSpec((S, W), lambda b, p, i: (b, COL_KB // SB_PAIRS + p)),
                  pl.BlockSpec((S, W), lambda b, p, i: (b, COL_VB // SB_PAIRS + p))],
        out_specs=pl.BlockSpec((SB_BLOCK, W), lambda b, p, i: (b * nq + i, p)),
        out_shape=jax.ShapeDtypeStruct((N, SB_W), f32),
        compiler_params=_params(("arbitrary", "arbitrary", "arbitrary")),
        name="sb",
    )(proj, proj, proj)


def _layer_norm(r, g, b):
    mu = jnp.mean(r, axis=-1, keepdims=True)
    d = r - mu
    var = jnp.mean(d * d, axis=-1, keepdims=True)
    return d * lax.rsqrt(var + LN_EPS) * g + b


def _merge_kernel(x_ref, oa_ref, ga_ref, ob_ref, gta0_ref, gta1_ref, gtb0_ref, gtb1_ref,
                  gt1_ref, sh2_ref, sc2_ref, hgn_ref, wua_ref, wub_ref, wo_ref, wpq_ref,
                  lng_ref, lnb_ref, x1_ref, hf_ref, pq_ref):
    oa = oa_ref[...]
    hgn = hgn_ref[...]
    segs = []
    for h in range(HG_HEADS):
        seg = oa[:, h * HG_D:(h + 1) * HG_D]
        ms = jnp.mean(seg * seg, axis=-1, keepdims=True)
        segs.append(seg * lax.rsqrt(ms + RMS_EPS) * hgn[:, h * HG_D:(h + 1) * HG_D])
    ga = ga_ref[...]
    oa_n = jnp.concatenate(segs, axis=-1) * (ga * jax.nn.sigmoid(ga))
    ma = _dot(oa_n.astype(bf16), wua_ref[...])
    mb = _dot(ob_ref[...].astype(bf16), wub_ref[...])
    gate_a = jnp.concatenate([gta0_ref[...], gta1_ref[...]], axis=-1)
    gate_b = jnp.concatenate([gtb0_ref[...], gtb1_ref[...]], axis=-1)
    merged = jax.nn.sigmoid(gate_a) * ma + jax.nn.sigmoid(gate_b) * mb
    y = _dot(merged.astype(bf16), wo_ref[...]) * gt1_ref[...]
    x1 = _layer_norm(DN_ALPHA * x_ref[...] + y, lng_ref[...], lnb_ref[...])
    x1_ref[...] = x1
    hf = x1 * (1.0 + sc2_ref[...]) + sh2_ref[...]
    tm = hf.shape[0]
    for r in range(SUBLANES):
        hf_ref[pl.ds(r, tm, stride=SUBLANES), :] = hf[:, r * LANES:(r + 1) * LANES]
    pq_ref[...] = _dot(hf.astype(bf16), wpq_ref[...])


def _merge_call(x, oa, proj, ob, mod6, hgn, wua, wub, wo, wpq, lng, lnb, S):
    N = x.shape[0]
    tm = 256
    tps = S // tm

    def full(shape):
        return pl.BlockSpec(shape, lambda i: (0,) * len(shape))

    def p512(c):
        return pl.BlockSpec((tm, 512), lambda i: (i, c))

    row = pl.BlockSpec((tm, D_MODEL), lambda i: (i, 0))
    out = jax.ShapeDtypeStruct((N, D_MODEL), f32)
    return pl.pallas_call(
        _merge_kernel,
        grid=(N // tm,),
        in_specs=[row, pl.BlockSpec((tm, HG_W), lambda i: (i, 0)), p512(COL512_GA),
                  pl.BlockSpec((tm, SB_W), lambda i: (i, 0)),
                  p512(COL512_GATE_A), p512(COL512_GATE_A + 1),
                  p512(COL512_GATE_B), p512(COL512_GATE_B + 1),
                  _mod_spec(2, tps), _mod_spec(3, tps), _mod_spec(4, tps),
                  full((1, HG_W)), full((HG_W, D_MODEL)), full((SB_W, D_MODEL)),
                  full((D_MODEL, D_MODEL)), full((D_MODEL, D_MODEL)),
                  full((1, D_MODEL)), full((1, D_MODEL))],
        out_specs=[row, pl.BlockSpec((tm * SUBLANES, LANES), lambda i: (i, 0)), row],
        out_shape=[out, jax.ShapeDtypeStruct((N * SUBLANES, LANES), f32), out],
        compiler_params=_params(("arbitrary",)),
        name="merge",
    )(x, oa, proj, ob, proj, proj, proj, proj, mod6, mod6, mod6, hgn, wua, wub, wo, wpq, lng, lnb)


TOPK_TOKENS = 256


def _top16(s, payload=None):
    R = s.shape[0]
    rid = lax.broadcasted_iota(i32, s.shape, 0).astype(f32)
    vals, ids = [], []
    for _ in range(PEER_TOPK):
        m = jnp.max(s, axis=0, keepdims=True)
        first = jnp.min(jnp.where(s == m, rid, float(R)), axis=0, keepdims=True)
        sel = rid == first
        vals.append(m)
        if payload is None:
            ids.append(first.astype(i32))
        else:
            ids.append(jnp.max(jnp.where(sel, payload, -1), axis=0, keepdims=True))
        s = jnp.where(sel, -jnp.inf, s)
    return vals, ids


def _topk_kernel(pq_ref, k1_ref, k2_ref, idx_ref, gate_ref, idx_t, gate_t):
    H = SUBLANES

    def head(h, _):
        c0 = pl.multiple_of(h * PEER_DKEY, PEER_DKEY)
        qh = pq_ref[:, pl.ds(c0, PEER_DKEY)].astype(bf16)
        s1 = _nt(k1_ref[h], qh)
        s2 = _nt(k2_ref[h], qh)
        v1, i1 = _top16(s1)
        v2, i2 = _top16(s2)
        v1a, i1a = jnp.concatenate(v1, axis=0), jnp.concatenate(i1, axis=0)
        v2a, i2a = jnp.concatenate(v2, axis=0), jnp.concatenate(i2, axis=0)
        cand = ([v1[0] + v2a[:H], v1[0] + v2a[H:]] + [v1[a] + v2a[:H] for a in range(1, H)]
                + [v1a[H:] + v2[0]])
        cidx = ([i1[0] * PEER_NKEYS + i2a[:H], i1[0] * PEER_NKEYS + i2a[H:]]
                + [i1[a] * PEER_NKEYS + i2a[:H] for a in range(1, H)] + [i1a[H:] * PEER_NKEYS + i2[0]])
        tv, ti = _top16(jnp.concatenate(cand, axis=0), jnp.concatenate(cidx, axis=0) * ROW_SUB)
        tva = jnp.concatenate(tv, axis=0)
        e = jnp.exp(tva - tv[0])
        r0 = pl.multiple_of(h * PEER_TOPK, PEER_TOPK)
        gate_t[pl.ds(r0, PEER_TOPK), :] = e / jnp.sum(e, axis=0, keepdims=True)
        idx_t[pl.ds(r0, PEER_TOPK), :] = jnp.concatenate(ti, axis=0)
        return 0

    lax.fori_loop(0, PEER_HEADS, head, 0)
    idx_ref[...] = idx_t[...].T
    gate_ref[...] = gate_t[...].T


def _topk_call(pq, k1p, k2p):
    N = pq.shape[0]
    T = TOPK_TOKENS
    keys = pl.BlockSpec((PEER_HEADS, PEER_NKEYS, PEER_DKEY), lambda i: (0, 0, 0))
    return pl.pallas_call(
        _topk_kernel,
        grid=(N // T,),
        in_specs=[pl.BlockSpec((T, D_MODEL), lambda i: (i, 0)), keys, keys],
        out_specs=[pl.BlockSpec((T, PEER_PAIRS), lambda i: (i, 0)),
                   pl.BlockSpec((T, PEER_PAIRS), lambda i: (i, 0))],
        out_shape=[jax.ShapeDtypeStruct((N, PEER_PAIRS), i32),
                   jax.ShapeDtypeStruct((N, PEER_PAIRS), f32)],
        scratch_shapes=[pltpu.VMEM((PEER_PAIRS, T), i32), pltpu.VMEM((PEER_PAIRS, T), f32)],
        compiler_params=_params(("arbitrary",)),
        name="topk",
    )(pq, k1p, k2p)


PEER_TOKENS = 64
PEER_GROUP = SUBLANES
STAGE_ROWS = PEER_PAIRS * ROW_SUB
STAGE_COLS = 2 * STAGE_ROWS
INV_SQRT2 = 1.0 / math.sqrt(2.0)


def _gather_rows(idx_s, k0, tab_ref, stage_ref):
    for k in range(PEER_PAIRS):
        e4 = pl.multiple_of(idx_s[k0 + k], ROW_SUB)
        stage_ref[k * ROW_SUB:(k + 1) * ROW_SUB, :] = tab_ref[pl.ds(e4, ROW_SUB), :]


def _piece_diag():
    sub = lax.broadcasted_iota(i32, (SUBLANES, STAGE_COLS), 0)
    col = lax.broadcasted_iota(i32, (SUBLANES, STAGE_COLS), 1)
    return (col & (SUBLANES - 1)) == 2 * (sub & (ROW_SUB - 1)) + lax.shift_right_logical(sub, 2)


def _peer_u_kernel(idx_s, hf_ref, g_ref, fold_ref, tab_ref, o_ref, stage, zs_ref):
    T = o_ref.shape[0]
    G = PEER_GROUP
    diag = _piece_diag()

    def group(gi, _):
        g0 = pl.multiple_of(gi * G, G)
        for t in range(G):
            _gather_rows(idx_s, (g0 + t) * PEER_PAIRS, tab_ref, stage)
            h_hi, h_lo = _split_bf16(hf_ref[pl.ds(pl.multiple_of((g0 + t) * SUBLANES, SUBLANES), SUBLANES), :])
            z = _nt(jnp.concatenate([h_hi, h_lo], axis=0), pltpu.bitcast(stage[...], bf16))
            zs_ref[t:t + 1, :] = jnp.sum(jnp.where(diag, z[:SUBLANES] + z[SUBLANES:], 0.0), axis=0, keepdims=True)
        zs_hi, zs_lo = _split_bf16(zs_ref[...])
        dots = _dot(zs_hi, fold_ref[...]) + _dot(zs_lo, fold_ref[...])
        act = 0.5 * dots * (1.0 + lax.erf(dots * INV_SQRT2))
        o_ref[pl.ds(g0, G), :] = act * g_ref[pl.ds(g0, G), :]
        return 0

    lax.fori_loop(0, T // G, group, 0)


def _peer_v_kernel(idx_s, w_ref, spread_ref, tab_ref, o_ref, stage):
    T = w_ref.shape[0]
    G = PEER_GROUP
    diag = _piece_diag()

    def group(gi, _):
        g0 = pl.multiple_of(gi * G, G)
        w8 = _dot(w_ref[pl.ds(g0, G), :].astype(bf16), spread_ref[...])
        for t in range(G):
            _gather_rows(idx_s, (g0 + t) * PEER_PAIRS, tab_ref, stage)
            wexp = jnp.where(diag, jnp.broadcast_to(w8[t:t + 1, :], diag.shape), 0.0).astype(bf16)
            r0 = pl.multiple_of((g0 + t) * SUBLANES, SUBLANES)
            o_ref[pl.ds(r0, SUBLANES), :] = _dot(wexp, pltpu.bitcast(stage[...], bf16))
        return 0

    lax.fori_loop(0, T // G, group, 0)


def _table_spec():
    return pl.BlockSpec((PEER_N * ROW_SUB, LANES), lambda i: (0, 0), pipeline_mode=pl.Buffered(1))


def _smem_flat(T, first):
    return pl.BlockSpec((T * PEER_PAIRS,), lambda i: (i + first,), memory_space=pltpu.SMEM)


def _pair_pieces():
    k = lax.broadcasted_iota(i32, (PEER_PAIRS, STAGE_COLS), 0)
    c = lax.broadcasted_iota(i32, (PEER_PAIRS, STAGE_COLS), 1)
    return (c // SUBLANES == k).astype(bf16)


def _peer_u_call(idx4, hf8, gates, u_pk, start, count):
    T = PEER_TOKENS
    first = start // T
    return pl.pallas_call(
        _peer_u_kernel,
        grid=(count // T,),
        in_specs=[_smem_flat(T, first),
                  pl.BlockSpec((T * SUBLANES, LANES), lambda i: (i + first, 0)),
                  pl.BlockSpec((T, PEER_PAIRS), lambda i: (i + first, 0)),
                  pl.BlockSpec((STAGE_COLS, PEER_PAIRS), lambda i: (0, 0)), _table_spec()],
        out_specs=pl.BlockSpec((T, PEER_PAIRS), lambda i: (i, 0)),
        out_shape=jax.ShapeDtypeStruct((count, PEER_PAIRS), f32),
        scratch_shapes=[pltpu.VMEM((STAGE_ROWS, LANES), i32), pltpu.VMEM((PEER_GROUP, STAGE_COLS), f32)],
        compiler_params=_params(("arbitrary",)),
        name="peer_u",
    )(idx4.reshape(-1), hf8, gates, _pair_pieces().T, u_pk)


def _peer_v_call(idx4, wgt, v_pk, start):
    count = wgt.shape[0]
    T = PEER_TOKENS
    return pl.pallas_call(
        _peer_v_kernel,
        grid=(count // T,),
        in_specs=[_smem_flat(T, start // T), pl.BlockSpec((T, PEER_PAIRS), lambda i: (i, 0)),
                  pl.BlockSpec((PEER_PAIRS, STAGE_COLS), lambda i: (0, 0)), _table_spec()],
        out_specs=pl.BlockSpec((T * SUBLANES, LANES), lambda i: (i, 0)),
        out_shape=jax.ShapeDtypeStruct((count * SUBLANES, LANES), f32),
        scratch_shapes=[pltpu.VMEM((STAGE_ROWS, LANES), i32)],
        compiler_params=_params(("arbitrary",)),
        name="peer_v",
    )(idx4.reshape(-1), wgt, _pair_pieces(), v_pk)


SC_CORES = 2
SC_SUBCORES = 16
SC_LANES = 16
SC_WORKERS = SC_CORES * SC_SUBCORES
SC_CHUNK = 16
SC_GROUP = 8
SC_PIECES = (15360, 25600)


def _sc_params():
    cp = pltpu.CompilerParams()
    if "needs_layout_passes" in pltpu.CompilerParams.__dataclass_fields__:
        cp = dataclasses.replace(cp, needs_layout_passes=False)
    return cp


def _peer_v_sc_call(idx, wgt, v3):
    n = idx.shape[0]
    per_worker = n // SC_WORKERS
    groups = LANES // SC_LANES
    n_chunks = PEER_PAIRS // SC_CHUNK
    mesh = plsc.VectorSubcoreMesh(core_axis_name="c", subcore_axis_name="s")
    rows_t = pltpu.VMEM((SC_CHUNK, SUBLANES, LANES), f32)

    @functools.partial(
        pl.kernel, mesh=mesh, compiler_params=_sc_params(),
        out_type=jax.ShapeDtypeStruct((n, SUBLANES, LANES), f32),
        scratch_types=[pltpu.VMEM((SC_GROUP, PEER_PAIRS), i32), pltpu.VMEM((SC_GROUP, PEER_PAIRS), f32),
                       rows_t, rows_t, pltpu.VMEM((SC_CHUNK, SC_LANES), f32),
                       pltpu.VMEM((SC_GROUP, SUBLANES, LANES), f32),
                       pltpu.SemaphoreType.DMA, pltpu.SemaphoreType.DMA])
    def sc_kernel(idx_hbm, w_hbm, tab_hbm, out_hbm, idx_v, w_v, rows_a, rows_b, wb_v, out_v, sem_a, sem_b):
        worker = lax.axis_index("s") * SC_CORES + lax.axis_index("c")
        zero = jnp.zeros((SC_LANES,), f32)
        bufs = ((rows_a, sem_a), (rows_b, sem_b))

        def gather(j, c):
            rows, sem = bufs[c % 2]
            return pltpu.make_async_copy(tab_hbm.at[idx_v.at[j, pl.ds(c * SC_CHUNK, SC_CHUNK)]], rows, sem)

        @pl.loop(0, per_worker // SC_GROUP)
        def _(g):
            t0 = worker * per_worker + g * SC_GROUP
            pltpu.sync_copy(idx_hbm.at[pl.ds(t0, SC_GROUP)], idx_v)
            pltpu.sync_copy(w_hbm.at[pl.ds(t0, SC_GROUP)], w_v)
            gather(0, 0).start()

            @pl.loop(0, SC_GROUP)
            def _(j):
                for c in range(n_chunks):
                    if c + 1 < n_chunks:
                        gather(j, c + 1).start()
                    else:
                        @pl.when(j + 1 < SC_GROUP)
                        def _():
                            gather(j + 1, 0).start()
                    gather(j, c).wait()
                    rows = bufs[c % 2][0]

                    @pl.loop(0, SC_CHUNK)
                    def _(r):
                        wb_v[r, :] = plsc.load_gather(
                            w_v, [jnp.full((SC_LANES,), 0, i32) + j, jnp.full((SC_LANES,), c * SC_CHUNK, i32) + r])

                    for s2 in range(SUBLANES // 2):
                        def where(a, s2=s2):
                            return 2 * s2 + a // groups, pl.ds((a % groups) * SC_LANES, SC_LANES)

                        def row_body(r, accs, rows=rows, where=where):
                            wv = wb_v[r, :]
                            return tuple(accs[a] + wv * rows[(r,) + where(a)] for a in range(2 * groups))

                        if c == 0:
                            init = (zero,) * (2 * groups)
                        else:
                            init = tuple(out_v[(j,) + where(a)] for a in range(2 * groups))
                        accs = lax.fori_loop(0, SC_CHUNK, row_body, init)
                        for a in range(2 * groups):
                            out_v[(j,) + where(a)] = accs[a]

            pltpu.sync_copy(out_v, out_hbm.at[pl.ds(t0, SC_GROUP)])

    return sc_kernel(idx, wgt, v3)


def _ln_kernel(x_ref, y_ref, gt_ref, g_ref, b_ref, o_ref):
    tm = x_ref.shape[0]
    y = jnp.concatenate([y_ref[pl.ds(r, tm, stride=SUBLANES), :] for r in range(SUBLANES)], axis=-1)
    o_ref[...] = _layer_norm(DN_ALPHA * x_ref[...] + y * gt_ref[...], g_ref[...], b_ref[...])


def _ln_call(x1, y8, mod6, lng, lnb, S):
    N = x1.shape[0]
    tm = 512
    row = pl.BlockSpec((tm, D_MODEL), lambda i: (i, 0))
    vec = pl.BlockSpec((1, D_MODEL), lambda i: (0, 0))
    return pl.pallas_call(
        _ln_kernel,
        grid=(N // tm,),
        in_specs=[row, pl.BlockSpec((tm * SUBLANES, LANES), lambda i: (i, 0)), _mod_spec(5, S // tm), vec, vec],
        out_specs=row,
        out_shape=jax.ShapeDtypeStruct((N, D_MODEL), f32),
        compiler_params=_params(("arbitrary",)),
        name="ln",
    )(x1, y8, mod6, lng, lnb)


def _pack_table(t):
    tb = lax.bitcast_convert_type(t.astype(bf16), jnp.uint16).astype(jnp.uint32)
    word = tb[:, :ROW_WORDS] | (tb[:, ROW_WORDS:] << 16)
    return lax.bitcast_convert_type(word, i32).reshape(PEER_N * ROW_SUB, LANES)


def _pad_keys(sub_keys_l):
    half = PEER_DKEY // 2
    z = jnp.zeros((PEER_HEADS, PEER_NKEYS, half), f32)
    k1 = jnp.concatenate([sub_keys_l[0], z], axis=-1).astype(bf16)
    k2 = jnp.concatenate([z, sub_keys_l[1]], axis=-1).astype(bf16)
    return k1, k2


def _layer(x, c, lb_logits, B, S, layer, p):
    (w_ada, b_ada, w_in, hgn, wua, wub, wo, wpq, sub_keys, pu, pv, ln_g, ln_b) = p
    mod6 = _mod_call(c, w_ada, b_ada.reshape(1, -1)).reshape(B, 6, 1, D_MODEL)
    proj = _proj_call(x, mod6, w_in.astype(bf16), S)
    oa = _hgrn_call(layer.reshape(1), proj, lb_logits, B, S)
    ob = _sb_call(proj, B, S)
    x1, hf8, pq = _merge_call(x, oa, proj, ob, mod6, hgn.reshape(1, -1), wua.astype(bf16),
                              wub.astype(bf16), wo.astype(bf16), wpq.astype(bf16),
                              ln_g[0:1], ln_b[0:1], S)
    k1p, k2p = _pad_keys(sub_keys)
    idx4, gates = _topk_call(pq, k1p, k2p)
    N = x.shape[0]
    u_pk = _pack_table(pu)
    v3 = pv.reshape(PEER_N, SUBLANES, LANES)
    pieces, start = [], 0
    for count in SC_PIECES:
        wgt = _peer_u_call(idx4, hf8, gates, u_pk, start, count)
        ids = lax.shift_right_logical(idx4[start:start + count], 2)
        pieces.append(_peer_v_sc_call(ids, wgt, v3).reshape(count * SUBLANES, LANES))
        start += count
    wgt = _peer_u_call(idx4, hf8, gates, u_pk, start, N - start)
    pieces.append(_peer_v_call(idx4, wgt, _pack_table(pv), start))
    y8 = jnp.concatenate(pieces, axis=0)
    return _ln_call(x1, y8, mod6, ln_g[1:2], ln_b[1:2], S)


def kernel(x, c, w_ada, b_ada, w_in, lb_logits, hg_norm_g, w_up_a, w_up_b, w_o, w_pq, sub_keys,
           peer_u, peer_v, ln_g, ln_b):
    B, S, _ = x.shape
    xs = (jnp.arange(DEPTH, dtype=i32), w_ada, b_ada, w_in, hg_norm_g, w_up_a, w_up_b, w_o, w_pq,
          sub_keys, peer_u, peer_v, ln_g, ln_b)

    def step(xc, per_layer):
        return _layer(xc, c, lb_logits, B, S, per_layer[0], per_layer[1:]), None

    out, _ = lax.scan(step, x.reshape(B * S, D_MODEL), xs)
    return out.reshape(B, S, D_MODEL)
```

```python
import dataclasses
import functools
import math

import jax
import jax.numpy as jnp
from jax import lax
from jax.experimental import pallas as pl
from jax.experimental.pallas import tpu as pltpu
from jax.experimental.pallas import tpu_sc as plsc

f32 = jnp.float32
bf16 = jnp.bfloat16
i32 = jnp.int32

D_MODEL = 1024
DEPTH = 4
HG_HEADS = 4
HG_D = 128
HG_W = HG_HEADS * HG_D
SB_HEADS = 8
SB_DH = 64
SB_W = SB_HEADS * SB_DH
SB_BLOCK = 128
PEER_HEADS = 8
PEER_NKEYS = 128
PEER_N = PEER_NKEYS * PEER_NKEYS
PEER_DKEY = 128
PEER_TOPK = 16
PEER_PAIRS = PEER_HEADS * PEER_TOPK
IN_WIDTH = 4 * HG_W + 3 * SB_W + 2 * D_MODEL
DN_ALPHA = (2.0 * DEPTH) ** 0.25
LN_EPS = 1e-5
RMS_EPS = 1e-6

LANES = 128
SUBLANES = 8
ROW_WORDS = D_MODEL // 2
ROW_SUB = ROW_WORDS // LANES
VMEM_LIMIT = 48 * 1024 * 1024

COL_QA, COL_FA, COL_IA = 0, 4, 8
COL_QB, COL_KB, COL_VB = 16, 20, 24
COL512_GA, COL512_GATE_A, COL512_GATE_B = 3, 7, 9


def _nt(a, b):
    return lax.dot_general(a, b, (((1,), (1,)), ((), ())), preferred_element_type=f32)


def _tn(a, b):
    return lax.dot_general(a, b, (((0,), (0,)), ((), ())), preferred_element_type=f32)


def _dot(a, b):
    return jnp.dot(a, b, preferred_element_type=f32)


def _split_bf16(a):
    hi = a.astype(bf16)
    lo = (a - hi.astype(f32)).astype(bf16)
    return hi, lo


def _params(sem):
    return pltpu.CompilerParams(dimension_semantics=sem, vmem_limit_bytes=VMEM_LIMIT)


def _mod_kernel(c_ref, w_ref, b_ref, o_ref):
    c = c_ref[...]
    cond = (c * jax.nn.sigmoid(c)).astype(bf16)
    o_ref[...] = _dot(cond, w_ref[...].astype(bf16)) + b_ref[...]


def _mod_call(c, w_ada_l, b_ada_l):
    B = c.shape[0]
    tn = 1536
    return pl.pallas_call(
        _mod_kernel,
        grid=(6 * D_MODEL // tn,),
        in_specs=[pl.BlockSpec((B, D_MODEL), lambda j: (0, 0)),
                  pl.BlockSpec((D_MODEL, tn), lambda j: (0, j)),
                  pl.BlockSpec((1, tn), lambda j: (0, j))],
        out_specs=pl.BlockSpec((B, tn), lambda j: (0, j)),
        out_shape=jax.ShapeDtypeStruct((B, 6 * D_MODEL), f32),
        compiler_params=_params(("arbitrary",)),
        name="mod",
    )(c, w_ada_l, b_ada_l)


def _mod_spec(which, tiles_per_seq):
    return pl.BlockSpec((None, None, 1, D_MODEL), lambda i: (i // tiles_per_seq, which, 0, 0))


def _proj_kernel(x_ref, sh_ref, sc_ref, w_ref, o_ref):
    hm = (x_ref[...] * (1.0 + sc_ref[...]) + sh_ref[...]).astype(bf16)
    for n0 in range(0, IN_WIDTH, 512):
        o_ref[:, n0:n0 + 512] = _dot(hm, w_ref[:, n0:n0 + 512])


def _proj_call(x, mod6, w_in_bf, S):
    N = x.shape[0]
    tm = 256
    tps = S // tm
    return pl.pallas_call(
        _proj_kernel,
        grid=(N // tm,),
        in_specs=[pl.BlockSpec((tm, D_MODEL), lambda i: (i, 0)),
                  _mod_spec(0, tps), _mod_spec(1, tps),
                  pl.BlockSpec((D_MODEL, IN_WIDTH), lambda i: (0, 0))],
        out_specs=pl.BlockSpec((tm, IN_WIDTH), lambda i: (i, 0)),
        out_shape=jax.ShapeDtypeStruct((N, IN_WIDTH), f32),
        compiler_params=_params(("arbitrary",)),
        name="proj",
    )(x, mod6, mod6, w_in_bf)


HG_CHUNK = 128
HG_SUB = 16
HG_STEP_ROWS = 512


def _hgrn_chunk(q, z, v, lb, st, tri):
    one_m_lb = 1.0 - lb
    g = jnp.log(lb + one_m_lb * jax.nn.sigmoid(z))
    k = one_m_lb * jax.nn.sigmoid(-z)
    g_hi, g_lo = _split_bf16(g)
    b = _dot(tri, g_hi) + _dot(tri, g_lo)
    b_end = b[HG_CHUNK - 1:HG_CHUNK, :]
    inter = _nt((q * jnp.exp(b)).astype(bf16), st.astype(bf16))
    v_bf = v.astype(bf16)
    row_c = lax.broadcasted_iota(i32, (HG_CHUNK, HG_D), 0)
    row_s = lax.broadcasted_iota(i32, (HG_SUB, HG_D), 0)
    blocks = []
    for sub in range(HG_CHUNK // HG_SUB):
        r0 = sub * HG_SUB
        bs = b[r0:r0 + HG_SUB]
        qs = q[r0:r0 + HG_SUB]
        ks = k[r0:r0 + HG_SUB]
        vs = v[r0:r0 + HG_SUB]
        rows = []
        for t in range(HG_SUB):
            m = row_s <= t
            e = jnp.where(m, jnp.exp(jnp.where(m, bs[t:t + 1] - bs, 0.0)), 0.0)
            p = (qs[t:t + 1] * ks) * e
            srow = jnp.sum(p, axis=-1, keepdims=True)
            rows.append(jnp.sum(srow * vs, axis=0, keepdims=True))
        o_sub = jnp.concatenate(rows, axis=0)
        if sub > 0:
            bref = b[r0 - 1:r0, :]
            qi = (qs * jnp.exp(bs - bref)).astype(bf16)
            past = row_c < r0
            ki = jnp.where(past, k * jnp.exp(jnp.where(past, bref - b, 0.0)), 0.0).astype(bf16)
            o_sub = o_sub + _dot(_nt(qi, ki).astype(bf16), v_bf)
        blocks.append(o_sub)
    intra = jnp.concatenate(blocks, axis=0)
    kd = (k * jnp.exp(b_end - b)).astype(bf16)
    st_new = st * jnp.exp(b_end) + _tn(v_bf, kd)
    return inter + intra, st_new


def _hgrn_kernel(l_ref, q_ref, z_ref, v_ref, lbl_ref, o_ref, st_ref):
    @pl.when(pl.program_id(2) == 0)
    def _():
        st_ref[...] = jnp.zeros_like(st_ref)

    logits = lbl_ref[...]
    e = jnp.exp(logits - jnp.max(logits, axis=0, keepdims=True))
    p = e / jnp.sum(e, axis=0, keepdims=True)
    rid = lax.broadcasted_iota(i32, p.shape, 0)
    l = l_ref[0]
    lb = jnp.sum(jnp.where((rid >= 1) & (rid <= l), p, 0.0), axis=0, keepdims=True)

    r = lax.broadcasted_iota(i32, (HG_CHUNK, HG_CHUNK), 0)
    c = lax.broadcasted_iota(i32, (HG_CHUNK, HG_CHUNK), 1)
    tri = jnp.where(c <= r, 1.0, 0.0).astype(bf16)

    def body(ci, st):
        r0 = pl.multiple_of(ci * HG_CHUNK, HG_CHUNK)
        out, st = _hgrn_chunk(q_ref[pl.ds(r0, HG_CHUNK), :], z_ref[pl.ds(r0, HG_CHUNK), :],
                              v_ref[pl.ds(r0, HG_CHUNK), :], lb, st, tri)
        o_ref[pl.ds(r0, HG_CHUNK), :] = out
        return st

    st_ref[...] = lax.fori_loop(0, HG_STEP_ROWS // HG_CHUNK, body, st_ref[...])


def _hgrn_call(layer, proj, lb_logits, B, S):
    N = proj.shape[0]
    R = HG_STEP_ROWS
    spb = S // R

    def col(c0):
        return pl.BlockSpec((R, HG_D), lambda b, h, s, l: (b * spb + s, c0 + h))

    grid_spec = pltpu.PrefetchScalarGridSpec(
        num_scalar_prefetch=1,
        grid=(B, HG_HEADS, spb),
        in_specs=[col(COL_QA), col(COL_FA), col(COL_IA),
                  pl.BlockSpec((DEPTH, HG_D), lambda b, h, s, l: (0, h))],
        out_specs=pl.BlockSpec((R, HG_D), lambda b, h, s, l: (b * spb + s, h)),
        scratch_shapes=[pltpu.VMEM((HG_D, HG_D), f32)],
    )
    return pl.pallas_call(
        _hgrn_kernel,
        grid_spec=grid_spec,
        out_shape=jax.ShapeDtypeStruct((N, HG_W), f32),
        compiler_params=_params(("arbitrary", "arbitrary", "arbitrary")),
        name="hgrn",
    )(layer, proj, proj, proj, lb_logits)


SB_SCALE = 1.0 / math.sqrt(SB_DH)
SB_PAIRS = 4
SB_GROUP = 3
SB_DEAD = -104.0
SB_NEVER = -(1 << 20)


def _sb_kernel(q_ref, k_ref, v_ref, o_ref):
    i = pl.program_id(2)
    T = SB_BLOCK
    P = SB_PAIRS
    lane = lax.broadcasted_iota(i32, (T, T), 1)
    lane2 = lax.broadcasted_iota(i32, (T, 2 * T), 1)
    row2 = lax.broadcasted_iota(i32, (T, 2 * T), 0)
    u = jnp.where((row2 > lane2) | (lane2 >= T), 1.0, 0.0).astype(bf16)
    qms = []
    for p in range(P):
        q2 = q_ref[:, p * T:(p + 1) * T] * SB_SCALE
        qms.append(jnp.concatenate([jnp.where(lane < SB_DH, q2, 0.0), jnp.where(lane >= SB_DH, q2, 0.0)],
                                   axis=0).astype(bf16))
    lane_s = lax.broadcasted_iota(i32, (2 * T, T), 1)
    row_s = lax.broadcasted_iota(i32, (2 * T, T), 0)
    key_minus_query = lane_s - (row_s & (T - 1))

    def trip(state):
        j_hi, carries, accs = state
        carries, accs = list(carries), list(accs)
        chains = [(p, b) for p in range(P) for b in range(SB_GROUP)]
        allowed, ks, vs = [], {}, {}
        for b in range(SB_GROUP):
            j = j_hi - b
            r0 = pl.multiple_of(jnp.maximum(j, 0) * T, T)
            allowed.append(key_minus_query < jnp.where(j >= 0, (i - j) * T, SB_NEVER))
            for p in range(P):
                ks[p, b] = k_ref[pl.ds(r0, T), p * T:(p + 1) * T].astype(bf16)
                vs[p, b] = v_ref[pl.ds(r0, T), p * T:(p + 1) * T].astype(bf16)
        zs = {c: _nt(qms[c[0]], ks[c]) for c in chains}
        lss, his, los = {}, {}, {}
        for c in chains:
            z = zs[c]
            lss[c] = jnp.minimum(z, 0.0) - jnp.log(1.0 + jnp.exp(-jnp.abs(z)))
            his[c], los[c] = _split_bf16(jnp.where(allowed[c[1]], lss[c] - z, 0.0))
        rs = {c: _dot(his[c], u) + _dot(los[c], u) for c in chains}
        ws = {}
        for p, b in chains:
            r = rs[p, b]
            ws[p, b] = jnp.where(allowed[b], jnp.exp(lss[p, b] + r[:, :T] + carries[p]), 0.0).astype(bf16)
            carries[p] = carries[p] + r[:, T:]
        for p, b in chains:
            accs[p] = accs[p] + _dot(ws[p, b], vs[p, b])
        return j_hi - SB_GROUP, tuple(carries), tuple(accs)

    def live(state):
        j_hi, carries, _ = state
        top = carries[0]
        for c in carries[1:]:
            top = jnp.maximum(top, c)
        return (j_hi >= 0) & (jnp.max(top) > SB_DEAD)

    zero = jnp.zeros((2 * T, T), f32)
    _, _, accs = lax.while_loop(live, trip, (i, (zero,) * P, (zero,) * P))
    for p in range(P):
        o_ref[:, p * T:(p + 1) * T] = jnp.where(lane < SB_DH, accs[p][:T], accs[p][T:])


def _sb_call(proj, B, S):
    N = proj.shape[0]
    nq = S // SB_BLOCK
    W = SB_PAIRS * LANES
    return pl.pallas_call(
        _sb_kernel,
        grid=(B, SB_HEADS // (2 * SB_PAIRS), nq),
        in_specs=[pl.BlockSpec((SB_BLOCK, W), lambda b, p, i: (b * nq + i, COL_QB // SB_PAIRS + p)),
                  pl.BlockSpec((S, W), lambda b, p, i: (b, COL_KB // SB_PAIRS + p)),
                  pl.BlockSpec((S, W), lambda b, p, i: (b, COL_VB // SB_PAIRS + p))],
        out_specs=pl.BlockSpec((SB_BLOCK, W), lambda b, p, i: (b * nq + i, p)),
        out_shape=jax.ShapeDtypeStruct((N, SB_W), f32),
        compiler_params=_params(("arbitrary", "arbitrary", "arbitrary")),
        name="sb",
    )(proj, proj, proj)


def _layer_norm(r, g, b):
    mu = jnp.mean(r, axis=-1, keepdims=True)
    d = r - mu
    var = jnp.mean(d * d, axis=-1, keepdims=True)
    return d * lax.rsqrt(var + LN_EPS) * g + b


def _merge_kernel(x_ref, oa_ref, ga_ref, ob_ref, gta0_ref, gta1_ref, gtb0_ref, gtb1_ref,
                  gt1_ref, sh2_ref, sc2_ref, hgn_ref, wua_ref, wub_ref, wo_ref, wpq_ref,
                  lng_ref, lnb_ref, x1_ref, hf_ref, pq_ref):
    oa = oa_ref[...]
    hgn = hgn_ref[...]
    segs = []
    for h in range(HG_HEADS):
        seg = oa[:, h * HG_D:(h + 1) * HG_D]
        ms = jnp.mean(seg * seg, axis=-1, keepdims=True)
        segs.append(seg * lax.rsqrt(ms + RMS_EPS) * hgn[:, h * HG_D:(h + 1) * HG_D])
    ga = ga_ref[...]
    oa_n = jnp.concatenate(segs, axis=-1) * (ga * jax.nn.sigmoid(ga))
    ma = _dot(oa_n.astype(bf16), wua_ref[...])
    mb = _dot(ob_ref[...].astype(bf16), wub_ref[...])
    gate_a = jnp.concatenate([gta0_ref[...], gta1_ref[...]], axis=-1)
    gate_b = jnp.concatenate([gtb0_ref[...], gtb1_ref[...]], axis=-1)
    merged = jax.nn.sigmoid(gate_a) * ma + jax.nn.sigmoid(gate_b) * mb
    y = _dot(merged.astype(bf16), wo_ref[...]) * gt1_ref[...]
    x1 = _layer_norm(DN_ALPHA * x_ref[...] + y, lng_ref[...], lnb_ref[...])
    x1_ref[...] = x1
    hf = x1 * (1.0 + sc2_ref[...]) + sh2_ref[...]
    tm = hf.shape[0]
    for r in range(SUBLANES):
        hf_ref[pl.ds(r, tm, stride=SUBLANES), :] = hf[:, r * LANES:(r + 1) * LANES]
    pq_ref[...] = _dot(hf.astype(bf16), wpq_ref[...])


def _merge_call(x, oa, proj, ob, mod6, hgn, wua, wub, wo, wpq, lng, lnb, S):
    N = x.shape[0]
    tm = 256
    tps = S // tm

    def full(shape):
        return pl.BlockSpec(shape, lambda i: (0,) * len(shape))

    def p512(c):
        return pl.BlockSpec((tm, 512), lambda i: (i, c))

    row = pl.BlockSpec((tm, D_MODEL), lambda i: (i, 0))
    out = jax.ShapeDtypeStruct((N, D_MODEL), f32)
    return pl.pallas_call(
        _merge_kernel,
        grid=(N // tm,),
        in_specs=[row, pl.BlockSpec((tm, HG_W), lambda i: (i, 0)), p512(COL512_GA),
                  pl.BlockSpec((tm, SB_W), lambda i: (i, 0)),
                  p512(COL512_GATE_A), p512(COL512_GATE_A + 1),
                  p512(COL512_GATE_B), p512(COL512_GATE_B + 1),
                  _mod_spec(2, tps), _mod_spec(3, tps), _mod_spec(4, tps),
                  full((1, HG_W)), full((HG_W, D_MODEL)), full((SB_W, D_MODEL)),
                  full((D_MODEL, D_MODEL)), full((D_MODEL, D_MODEL)),
                  full((1, D_MODEL)), full((1, D_MODEL))],
        out_specs=[row, pl.BlockSpec((tm * SUBLANES, LANES), lambda i: (i, 0)), row],
        out_shape=[out, jax.ShapeDtypeStruct((N * SUBLANES, LANES), f32), out],
        compiler_params=_params(("arbitrary",)),
        name="merge",
    )(x, oa, proj, ob, proj, proj, proj, proj, mod6, mod6, mod6, hgn, wua, wub, wo, wpq, lng, lnb)


TOPK_TOKENS = 256


def _top16(s, payload=None):
    R = s.shape[0]
    rid = lax.broadcasted_iota(i32, s.shape, 0).astype(f32)
    vals, ids = [], []
    for _ in range(PEER_TOPK):
        m = jnp.max(s, axis=0, keepdims=True)
        first = jnp.min(jnp.where(s == m, rid, float(R)), axis=0, keepdims=True)
        sel = rid == first
        vals.append(m)
        if payload is None:
            ids.append(first.astype(i32))
        else:
            ids.append(jnp.max(jnp.where(sel, payload, -1), axis=0, keepdims=True))
        s = jnp.where(sel, -jnp.inf, s)
    return vals, ids


def _topk_kernel(pq_ref, k1_ref, k2_ref, after_ref, idx_ref, gate_ref, idx_t, gate_t):
    del after_ref
    H = SUBLANES

    def head(h, _):
        c0 = pl.multiple_of(h * PEER_DKEY, PEER_DKEY)
        qh = pq_ref[:, pl.ds(c0, PEER_DKEY)].astype(bf16)
        s1 = _nt(k1_ref[h], qh)
        s2 = _nt(k2_ref[h], qh)
        v1, i1 = _top16(s1)
        v2, i2 = _top16(s2)
        v1a, i1a = jnp.concatenate(v1, axis=0), jnp.concatenate(i1, axis=0)
        v2a, i2a = jnp.concatenate(v2, axis=0), jnp.concatenate(i2, axis=0)
        cand = ([v1[0] + v2a[:H], v1[0] + v2a[H:]] + [v1[a] + v2a[:H] for a in range(1, H)]
                + [v1a[H:] + v2[0]])
        cidx = ([i1[0] * PEER_NKEYS + i2a[:H], i1[0] * PEER_NKEYS + i2a[H:]]
                + [i1[a] * PEER_NKEYS + i2a[:H] for a in range(1, H)] + [i1a[H:] * PEER_NKEYS + i2[0]])
        tv, ti = _top16(jnp.concatenate(cand, axis=0), jnp.concatenate(cidx, axis=0) * ROW_SUB)
        tva = jnp.concatenate(tv, axis=0)
        e = jnp.exp(tva - tv[0])
        r0 = pl.multiple_of(h * PEER_TOPK, PEER_TOPK)
        gate_t[pl.ds(r0, PEER_TOPK), :] = e / jnp.sum(e, axis=0, keepdims=True)
        idx_t[pl.ds(r0, PEER_TOPK), :] = jnp.concatenate(ti, axis=0)
        return 0

    lax.fori_loop(0, PEER_HEADS, head, 0)
    idx_ref[...] = idx_t[...].T
    gate_ref[...] = gate_t[...].T


def _topk_call(pq, k1p, k2p, start, count, after):
    T = TOPK_TOKENS
    first = start // T
    keys = pl.BlockSpec((PEER_HEADS, PEER_NKEYS, PEER_DKEY), lambda i: (0, 0, 0))
    return pl.pallas_call(
        _topk_kernel,
        grid=(count // T,),
        in_specs=[pl.BlockSpec((T, D_MODEL), lambda i: (i + first, 0)), keys, keys,
                  pl.BlockSpec(memory_space=pl.ANY)],
        out_specs=[pl.BlockSpec((T, PEER_PAIRS), lambda i: (i, 0)),
                   pl.BlockSpec((T, PEER_PAIRS), lambda i: (i, 0))],
        out_shape=[jax.ShapeDtypeStruct((count, PEER_PAIRS), i32),
                   jax.ShapeDtypeStruct((count, PEER_PAIRS), f32)],
        scratch_shapes=[pltpu.VMEM((PEER_PAIRS, T), i32), pltpu.VMEM((PEER_PAIRS, T), f32)],
        compiler_params=_params(("arbitrary",)),
        name="topk",
    )(pq, k1p, k2p, after)


PEER_TOKENS = 64
PEER_GROUP = SUBLANES
STAGE_ROWS = PEER_PAIRS * ROW_SUB
STAGE_COLS = 2 * STAGE_ROWS
INV_SQRT2 = 1.0 / math.sqrt(2.0)


def _gather_rows(idx_s, k0, tab_ref, stage_ref):
    for k in range(PEER_PAIRS):
        e4 = pl.multiple_of(idx_s[k0 + k], ROW_SUB)
        stage_ref[k * ROW_SUB:(k + 1) * ROW_SUB, :] = tab_ref[pl.ds(e4, ROW_SUB), :]


def _piece_diag():
    sub = lax.broadcasted_iota(i32, (SUBLANES, STAGE_COLS), 0)
    col = lax.broadcasted_iota(i32, (SUBLANES, STAGE_COLS), 1)
    return (col & (SUBLANES - 1)) == 2 * (sub & (ROW_SUB - 1)) + lax.shift_right_logical(sub, 2)


def _peer_u_kernel(idx_s, hf_ref, g_ref, fold_ref, tab_ref, o_ref, stage, zs_ref):
    T = o_ref.shape[0]
    G = PEER_GROUP
    diag = _piece_diag()

    def group(gi, _):
        g0 = pl.multiple_of(gi * G, G)
        for t in range(G):
            _gather_rows(idx_s, (g0 + t) * PEER_PAIRS, tab_ref, stage)
            h_hi, h_lo = _split_bf16(hf_ref[pl.ds(pl.multiple_of((g0 + t) * SUBLANES, SUBLANES), SUBLANES), :])
            z = _nt(jnp.concatenate([h_hi, h_lo], axis=0), pltpu.bitcast(stage[...], bf16))
            zs_ref[t:t + 1, :] = jnp.sum(jnp.where(diag, z[:SUBLANES] + z[SUBLANES:], 0.0), axis=0, keepdims=True)
        zs_hi, zs_lo = _split_bf16(zs_ref[...])
        dots = _dot(zs_hi, fold_ref[...]) + _dot(zs_lo, fold_ref[...])
        act = 0.5 * dots * (1.0 + lax.erf(dots * INV_SQRT2))
        o_ref[pl.ds(g0, G), :] = act * g_ref[pl.ds(g0, G), :]
        return 0

    lax.fori_loop(0, T // G, group, 0)


def _peer_v_kernel(idx_s, w_ref, spread_ref, tab_ref, o_ref, stage):
    T = w_ref.shape[0]
    G = PEER_GROUP
    diag = _piece_diag()

    def group(gi, _):
        g0 = pl.multiple_of(gi * G, G)
        w8 = _dot(w_ref[pl.ds(g0, G), :].astype(bf16), spread_ref[...])
        for t in range(G):
            _gather_rows(idx_s, (g0 + t) * PEER_PAIRS, tab_ref, stage)
            wexp = jnp.where(diag, jnp.broadcast_to(w8[t:t + 1, :], diag.shape), 0.0).astype(bf16)
            r0 = pl.multiple_of((g0 + t) * SUBLANES, SUBLANES)
            o_ref[pl.ds(r0, SUBLANES), :] = _dot(wexp, pltpu.bitcast(stage[...], bf16))
        return 0

    lax.fori_loop(0, T // G, group, 0)


def _table_spec():
    return pl.BlockSpec((PEER_N * ROW_SUB, LANES), lambda i: (0, 0), pipeline_mode=pl.Buffered(1))


def _smem_flat(T, first):
    return pl.BlockSpec((T * PEER_PAIRS,), lambda i: (i + first,), memory_space=pltpu.SMEM)


def _pair_pieces():
    k = lax.broadcasted_iota(i32, (PEER_PAIRS, STAGE_COLS), 0)
    c = lax.broadcasted_iota(i32, (PEER_PAIRS, STAGE_COLS), 1)
    return (c // SUBLANES == k).astype(bf16)


def _peer_u_call(idx4, hf8, gates, u_pk, local, start, count):
    T = PEER_TOKENS
    first = start // T
    first_local = local // T
    return pl.pallas_call(
        _peer_u_kernel,
        grid=(count // T,),
        in_specs=[_smem_flat(T, first_local),
                  pl.BlockSpec((T * SUBLANES, LANES), lambda i: (i + first, 0)),
                  pl.BlockSpec((T, PEER_PAIRS), lambda i: (i + first_local, 0)),
                  pl.BlockSpec((STAGE_COLS, PEER_PAIRS), lambda i: (0, 0)), _table_spec()],
        out_specs=pl.BlockSpec((T, PEER_PAIRS), lambda i: (i, 0)),
        out_shape=jax.ShapeDtypeStruct((count, PEER_PAIRS), f32),
        scratch_shapes=[pltpu.VMEM((STAGE_ROWS, LANES), i32), pltpu.VMEM((PEER_GROUP, STAGE_COLS), f32)],
        compiler_params=_params(("arbitrary",)),
        name="peer_u",
    )(idx4.reshape(-1), hf8, gates, _pair_pieces().T, u_pk)


def _peer_v_call(idx4, wgt, v_pk, start):
    count = wgt.shape[0]
    T = PEER_TOKENS
    return pl.pallas_call(
        _peer_v_kernel,
        grid=(count // T,),
        in_specs=[_smem_flat(T, start // T), pl.BlockSpec((T, PEER_PAIRS), lambda i: (i, 0)),
                  pl.BlockSpec((PEER_PAIRS, STAGE_COLS), lambda i: (0, 0)), _table_spec()],
        out_specs=pl.BlockSpec((T * SUBLANES, LANES), lambda i: (i, 0)),
        out_shape=jax.ShapeDtypeStruct((count * SUBLANES, LANES), f32),
        scratch_shapes=[pltpu.VMEM((STAGE_ROWS, LANES), i32)],
        compiler_params=_params(("arbitrary",)),
        name="peer_v",
    )(idx4.reshape(-1), wgt, _pair_pieces(), v_pk)


SC_CORES = 2
SC_SUBCORES = 16
SC_LANES = 16
SC_WORKERS = SC_CORES * SC_SUBCORES
SC_CHUNK = 16
SC_GROUP = 8
SC_PIECES = (22528, 22016)


def _sc_params():
    cp = pltpu.CompilerParams()
    if "needs_layout_passes" in pltpu.CompilerParams.__dataclass_fields__:
        cp = dataclasses.replace(cp, needs_layout_passes=False)
    return cp


def _peer_v_sc_call(idx, wgt, v3):
    n = idx.shape[0]
    per_worker = n // SC_WORKERS
    groups = LANES // SC_LANES
    n_chunks = PEER_PAIRS // SC_CHUNK
    mesh = plsc.VectorSubcoreMesh(core_axis_name="c", subcore_axis_name="s")
    rows_t = pltpu.VMEM((SC_CHUNK, SUBLANES, LANES), f32)

    @functools.partial(
        pl.kernel, mesh=mesh, compiler_params=_sc_params(),
        out_type=jax.ShapeDtypeStruct((n, SUBLANES, LANES), f32),
        scratch_types=[pltpu.VMEM((SC_GROUP, PEER_PAIRS), i32), pltpu.VMEM((SC_GROUP, PEER_PAIRS), f32),
                       rows_t, rows_t, pltpu.VMEM((SC_CHUNK, SC_LANES), f32),
                       pltpu.VMEM((SC_GROUP, SUBLANES, LANES), f32),
                       pltpu.SemaphoreType.DMA, pltpu.SemaphoreType.DMA])
    def sc_kernel(idx_hbm, w_hbm, tab_hbm, out_hbm, idx_v, w_v, rows_a, rows_b, wb_v, out_v, sem_a, sem_b):
        worker = lax.axis_index("s") * SC_CORES + lax.axis_index("c")
        zero = jnp.zeros((SC_LANES,), f32)
        bufs = ((rows_a, sem_a), (rows_b, sem_b))

        def gather(j, c):
            rows, sem = bufs[c % 2]
            return pltpu.make_async_copy(tab_hbm.at[idx_v.at[j, pl.ds(c * SC_CHUNK, SC_CHUNK)]], rows, sem)

        @pl.loop(0, per_worker // SC_GROUP)
        def _(g):
            t0 = worker * per_worker + g * SC_GROUP
            pltpu.sync_copy(idx_hbm.at[pl.ds(t0, SC_GROUP)], idx_v)
            pltpu.sync_copy(w_hbm.at[pl.ds(t0, SC_GROUP)], w_v)
            gather(0, 0).start()

            @pl.loop(0, SC_GROUP)
            def _(j):
                for c in range(n_chunks):
                    if c + 1 < n_chunks:
                        gather(j, c + 1).start()
                    else:
                        @pl.when(j + 1 < SC_GROUP)
                        def _():
                            gather(j + 1, 0).start()
                    gather(j, c).wait()
                    rows = bufs[c % 2][0]

                    @pl.loop(0, SC_CHUNK)
                    def _(r):
                        wb_v[r, :] = plsc.load_gather(
                            w_v, [jnp.full((SC_LANES,), 0, i32) + j, jnp.full((SC_LANES,), c * SC_CHUNK, i32) + r])

                    for s2 in range(SUBLANES // 2):
                        def where(a, s2=s2):
                            return 2 * s2 + a // groups, pl.ds((a % groups) * SC_LANES, SC_LANES)

                        def row_body(r, accs, rows=rows, where=where):
                            wv = wb_v[r, :]
                            return tuple(accs[a] + wv * rows[(r,) + where(a)] for a in range(2 * groups))

                        if c == 0:
                            init = (zero,) * (2 * groups)
                        else:
                            init = tuple(out_v[(j,) + where(a)] for a in range(2 * groups))
                        accs = lax.fori_loop(0, SC_CHUNK, row_body, init)
                        for a in range(2 * groups):
                            out_v[(j,) + where(a)] = accs[a]

            pltpu.sync_copy(out_v, out_hbm.at[pl.ds(t0, SC_GROUP)])

    return sc_kernel(idx, wgt, v3)


def _ln_kernel(x_ref, y_ref, gt_ref, g_ref, b_ref, o_ref):
    tm = x_ref.shape[0]
    y = jnp.concatenate([y_ref[pl.ds(r, tm, stride=SUBLANES), :] for r in range(SUBLANES)], axis=-1)
    o_ref[...] = _layer_norm(DN_ALPHA * x_ref[...] + y * gt_ref[...], g_ref[...], b_ref[...])


def _ln_call(x1, y8, mod6, lng, lnb, S):
    N = x1.shape[0]
    tm = 512
    row = pl.BlockSpec((tm, D_MODEL), lambda i: (i, 0))
    vec = pl.BlockSpec((1, D_MODEL), lambda i: (0, 0))
    return pl.pallas_call(
        _ln_kernel,
        grid=(N // tm,),
        in_specs=[row, pl.BlockSpec((tm * SUBLANES, LANES), lambda i: (i, 0)), _mod_spec(5, S // tm), vec, vec],
        out_specs=row,
        out_shape=jax.ShapeDtypeStruct((N, D_MODEL), f32),
        compiler_params=_params(("arbitrary",)),
        name="ln",
    )(x1, y8, mod6, lng, lnb)


def _pack_table(t):
    tb = lax.bitcast_convert_type(t.astype(bf16), jnp.uint16).astype(jnp.uint32)
    word = tb[:, :ROW_WORDS] | (tb[:, ROW_WORDS:] << 16)
    return lax.bitcast_convert_type(word, i32).reshape(PEER_N * ROW_SUB, LANES)


def _pad_keys(sub_keys_l):
    half = PEER_DKEY // 2
    z = jnp.zeros((PEER_HEADS, PEER_NKEYS, half), f32)
    k1 = jnp.concatenate([sub_keys_l[0], z], axis=-1).astype(bf16)
    k2 = jnp.concatenate([z, sub_keys_l[1]], axis=-1).astype(bf16)
    return k1, k2


def _layer(x, c, lb_logits, B, S, layer, p):
    (w_ada, b_ada, w_in, hgn, wua, wub, wo, wpq, sub_keys, pu, pv, ln_g, ln_b) = p
    mod6 = _mod_call(c, w_ada, b_ada.reshape(1, -1)).reshape(B, 6, 1, D_MODEL)
    proj = _proj_call(x, mod6, w_in.astype(bf16), S)
    oa = _hgrn_call(layer.reshape(1), proj, lb_logits, B, S)
    ob = _sb_call(proj, B, S)
    x1, hf8, pq = _merge_call(x, oa, proj, ob, mod6, hgn.reshape(1, -1), wua.astype(bf16),
                              wub.astype(bf16), wo.astype(bf16), wpq.astype(bf16),
                              ln_g[0:1], ln_b[0:1], S)
    k1p, k2p = _pad_keys(sub_keys)
    N = x.shape[0]
    u_pk = _pack_table(pu)
    v3 = pv.reshape(PEER_N, SUBLANES, LANES)
    n0 = SC_PIECES[0]
    idx_0, gates_0 = _topk_call(pq, k1p, k2p, 0, n0, k1p)
    wgt_0 = _peer_u_call(idx_0, hf8, gates_0, u_pk, 0, 0, n0)
    pieces = [_peer_v_sc_call(lax.shift_right_logical(idx_0, 2), wgt_0, v3).reshape(n0 * SUBLANES, LANES)]
    idx_r, gates_r = _topk_call(pq, k1p, k2p, n0, N - n0, wgt_0)
    local = 0
    for count in SC_PIECES[1:]:
        wgt = _peer_u_call(idx_r, hf8, gates_r, u_pk, local, n0 + local, count)
        ids = lax.shift_right_logical(idx_r[local:local + count], 2)
        pieces.append(_peer_v_sc_call(ids, wgt, v3).reshape(count * SUBLANES, LANES))
        local += count
    wgt = _peer_u_call(idx_r, hf8, gates_r, u_pk, local, n0 + local, N - n0 - local)
    pieces.append(_peer_v_call(idx_r, wgt, _pack_table(pv), local))
    y8 = jnp.concatenate(pieces, axis=0)
    return _ln_call(x1, y8, mod6, ln_g[1:2], ln_b[1:2], S)


def kernel(x, c, w_ada, b_ada, w_in, lb_logits, hg_norm_g, w_up_a, w_up_b, w_o, w_pq, sub_keys,
           peer_u, peer_v, ln_g, ln_b):
    B, S, _ = x.shape
    xs = (jnp.arange(DEPTH, dtype=i32), w_ada, b_ada, w_in, hg_norm_g, w_up_a, w_up_b, w_o, w_pq,
          sub_keys, peer_u, peer_v, ln_g, ln_b)

    def step(xc, per_layer):
        return _layer(xc, c, lb_logits, B, S, per_layer[0], per_layer[1:]), None

    out, _ = lax.scan(step, x.reshape(B * S, D_MODEL), xs)
    return out.reshape(B, S, D_MODEL)
```

```python
import dataclasses
import functools
import math

import jax
import jax.numpy as jnp
from jax import lax
from jax.experimental import pallas as pl
from jax.experimental.pallas import tpu as pltpu
from jax.experimental.pallas import tpu_sc as plsc

f32 = jnp.float32
bf16 = jnp.bfloat16
i32 = jnp.int32

D_MODEL = 1024
DEPTH = 4
HG_HEADS = 4
HG_D = 128
HG_W = HG_HEADS * HG_D
SB_HEADS = 8
SB_DH = 64
SB_W = SB_HEADS * SB_DH
SB_BLOCK = 128
PEER_HEADS = 8
PEER_NKEYS = 128
PEER_N = PEER_NKEYS * PEER_NKEYS
PEER_DKEY = 128
PEER_TOPK = 16
PEER_PAIRS = PEER_HEADS * PEER_TOPK
IN_WIDTH = 4 * HG_W + 3 * SB_W + 2 * D_MODEL
DN_ALPHA = (2.0 * DEPTH) ** 0.25
LN_EPS = 1e-5
RMS_EPS = 1e-6

LANES = 128
SUBLANES = 8
ROW_WORDS = D_MODEL // 2
ROW_SUB = ROW_WORDS // LANES
VMEM_LIMIT = 48 * 1024 * 1024

COL_QA, COL_FA, COL_IA = 0, 4, 8
COL_QB, COL_KB, COL_VB = 16, 20, 24
COL512_GA, COL512_GATE_A, COL512_GATE_B = 3, 7, 9


def _nt(a, b):
    return lax.dot_general(a, b, (((1,), (1,)), ((), ())), preferred_element_type=f32)


def _tn(a, b):
    return lax.dot_general(a, b, (((0,), (0,)), ((), ())), preferred_element_type=f32)


def _dot(a, b):
    return jnp.dot(a, b, preferred_element_type=f32)


def _split_bf16(a):
    hi = a.astype(bf16)
    lo = (a - hi.astype(f32)).astype(bf16)
    return hi, lo


def _params(sem):
    return pltpu.CompilerParams(dimension_semantics=sem, vmem_limit_bytes=VMEM_LIMIT)


def _mod_kernel(c_ref, w_ref, b_ref, o_ref):
    c = c_ref[...]
    cond = (c * jax.nn.sigmoid(c)).astype(bf16)
    o_ref[...] = _dot(cond, w_ref[...].astype(bf16)) + b_ref[...]


def _mod_call(c, w_ada_l, b_ada_l):
    B = c.shape[0]
    tn = 1536
    return pl.pallas_call(
        _mod_kernel,
        grid=(6 * D_MODEL // tn,),
        in_specs=[pl.BlockSpec((B, D_MODEL), lambda j: (0, 0)),
                  pl.BlockSpec((D_MODEL, tn), lambda j: (0, j)),
                  pl.BlockSpec((1, tn), lambda j: (0, j))],
        out_specs=pl.BlockSpec((B, tn), lambda j: (0, j)),
        out_shape=jax.ShapeDtypeStruct((B, 6 * D_MODEL), f32),
        compiler_params=_params(("arbitrary",)),
        name="mod",
    )(c, w_ada_l, b_ada_l)


def _mod_spec(which, tiles_per_seq):
    return pl.BlockSpec((None, None, 1, D_MODEL), lambda i: (i // tiles_per_seq, which, 0, 0))


def _proj_kernel(x_ref, sh_ref, sc_ref, w_ref, o_ref):
    hm = (x_ref[...] * (1.0 + sc_ref[...]) + sh_ref[...]).astype(bf16)
    for n0 in range(0, IN_WIDTH, 512):
        o_ref[:, n0:n0 + 512] = _dot(hm, w_ref[:, n0:n0 + 512])


def _proj_call(x, mod6, w_in_bf, S):
    N = x.shape[0]
    tm = 256
    tps = S // tm
    return pl.pallas_call(
        _proj_kernel,
        grid=(N // tm,),
        in_specs=[pl.BlockSpec((tm, D_MODEL), lambda i: (i, 0)),
                  _mod_spec(0, tps), _mod_spec(1, tps),
                  pl.BlockSpec((D_MODEL, IN_WIDTH), lambda i: (0, 0))],
        out_specs=pl.BlockSpec((tm, IN_WIDTH), lambda i: (i, 0)),
        out_shape=jax.ShapeDtypeStruct((N, IN_WIDTH), f32),
        compiler_params=_params(("arbitrary",)),
        name="proj",
    )(x, mod6, mod6, w_in_bf)


HG_CHUNK = 128
HG_SUB = 16
HG_STEP_ROWS = 512


def _hgrn_chunk(q, z, v, lb, st, tri):
    one_m_lb = 1.0 - lb
    g = jnp.log(lb + one_m_lb * jax.nn.sigmoid(z))
    k = one_m_lb * jax.nn.sigmoid(-z)
    g_hi, g_lo = _split_bf16(g)
    b = _dot(tri, g_hi) + _dot(tri, g_lo)
    b_end = b[HG_CHUNK - 1:HG_CHUNK, :]
    inter = _nt((q * jnp.exp(b)).astype(bf16), st.astype(bf16))
    v_bf = v.astype(bf16)
    row_c = lax.broadcasted_iota(i32, (HG_CHUNK, HG_D), 0)
    row_s = lax.broadcasted_iota(i32, (HG_SUB, HG_D), 0)
    blocks = []
    for sub in range(HG_CHUNK // HG_SUB):
        r0 = sub * HG_SUB
        bs = b[r0:r0 + HG_SUB]
        qs = q[r0:r0 + HG_SUB]
        ks = k[r0:r0 + HG_SUB]
        vs = v[r0:r0 + HG_SUB]
        rows = []
        for t in range(HG_SUB):
            m = row_s <= t
            e = jnp.where(m, jnp.exp(jnp.where(m, bs[t:t + 1] - bs, 0.0)), 0.0)
            p = (qs[t:t + 1] * ks) * e
            srow = jnp.sum(p, axis=-1, keepdims=True)
            rows.append(jnp.sum(srow * vs, axis=0, keepdims=True))
        o_sub = jnp.concatenate(rows, axis=0)
        if sub > 0:
            bref = b[r0 - 1:r0, :]
            qi = (qs * jnp.exp(bs - bref)).astype(bf16)
            past = row_c < r0
            ki = jnp.where(past, k * jnp.exp(jnp.where(past, bref - b, 0.0)), 0.0).astype(bf16)
            o_sub = o_sub + _dot(_nt(qi, ki).astype(bf16), v_bf)
        blocks.append(o_sub)
    intra = jnp.concatenate(blocks, axis=0)
    kd = (k * jnp.exp(b_end - b)).astype(bf16)
    st_new = st * jnp.exp(b_end) + _tn(v_bf, kd)
    return inter + intra, st_new


def _hgrn_kernel(l_ref, q_ref, z_ref, v_ref, lbl_ref, o_ref, st_ref):
    @pl.when(pl.program_id(2) == 0)
    def _():
        st_ref[...] = jnp.zeros_like(st_ref)

    logits = lbl_ref[...]
    e = jnp.exp(logits - jnp.max(logits, axis=0, keepdims=True))
    p = e / jnp.sum(e, axis=0, keepdims=True)
    rid = lax.broadcasted_iota(i32, p.shape, 0)
    l = l_ref[0]
    lb = jnp.sum(jnp.where((rid >= 1) & (rid <= l), p, 0.0), axis=0, keepdims=True)

    r = lax.broadcasted_iota(i32, (HG_CHUNK, HG_CHUNK), 0)
    c = lax.broadcasted_iota(i32, (HG_CHUNK, HG_CHUNK), 1)
    tri = jnp.where(c <= r, 1.0, 0.0).astype(bf16)

    def body(ci, st):
        r0 = pl.multiple_of(ci * HG_CHUNK, HG_CHUNK)
        out, st = _hgrn_chunk(q_ref[pl.ds(r0, HG_CHUNK), :], z_ref[pl.ds(r0, HG_CHUNK), :],
                              v_ref[pl.ds(r0, HG_CHUNK), :], lb, st, tri)
        o_ref[pl.ds(r0, HG_CHUNK), :] = out
        return st

    st_ref[...] = lax.fori_loop(0, HG_STEP_ROWS // HG_CHUNK, body, st_ref[...])


def _hgrn_call(layer, proj, lb_logits, B, S):
    N = proj.shape[0]
    R = HG_STEP_ROWS
    spb = S // R

    def col(c0):
        return pl.BlockSpec((R, HG_D), lambda b, h, s, l: (b * spb + s, c0 + h))

    grid_spec = pltpu.PrefetchScalarGridSpec(
        num_scalar_prefetch=1,
        grid=(B, HG_HEADS, spb),
        in_specs=[col(COL_QA), col(COL_FA), col(COL_IA),
                  pl.BlockSpec((DEPTH, HG_D), lambda b, h, s, l: (0, h))],
        out_specs=pl.BlockSpec((R, HG_D), lambda b, h, s, l: (b * spb + s, h)),
        scratch_shapes=[pltpu.VMEM((HG_D, HG_D), f32)],
    )
    return pl.pallas_call(
        _hgrn_kernel,
        grid_spec=grid_spec,
        out_shape=jax.ShapeDtypeStruct((N, HG_W), f32),
        compiler_params=_params(("arbitrary", "arbitrary", "arbitrary")),
        name="hgrn",
    )(layer, proj, proj, proj, lb_logits)


SB_SCALE = 1.0 / math.sqrt(SB_DH)
SB_PAIRS = 4
SB_GROUP = 3
SB_DEAD = -104.0
SB_NEVER = -(1 << 20)


def _sb_kernel(q_ref, k_ref, v_ref, o_ref):
    i = pl.program_id(2)
    T = SB_BLOCK
    P = SB_PAIRS
    lane = lax.broadcasted_iota(i32, (T, T), 1)
    lane2 = lax.broadcasted_iota(i32, (T, 2 * T), 1)
    row2 = lax.broadcasted_iota(i32, (T, 2 * T), 0)
    u = jnp.where((row2 > lane2) | (lane2 >= T), 1.0, 0.0).astype(bf16)
    qms = []
    for p in range(P):
        q2 = q_ref[:, p * T:(p + 1) * T] * SB_SCALE
        qms.append(jnp.concatenate([jnp.where(lane < SB_DH, q2, 0.0), jnp.where(lane >= SB_DH, q2, 0.0)],
                                   axis=0).astype(bf16))
    lane_s = lax.broadcasted_iota(i32, (2 * T, T), 1)
    row_s = lax.broadcasted_iota(i32, (2 * T, T), 0)
    key_minus_query = lane_s - (row_s & (T - 1))

    def trip(state):
        j_hi, carries, accs = state
        carries, accs = list(carries), list(accs)
        chains = [(p, b) for p in range(P) for b in range(SB_GROUP)]
        allowed, ks, vs = [], {}, {}
        for b in range(SB_GROUP):
            j = j_hi - b
            r0 = pl.multiple_of(jnp.maximum(j, 0) * T, T)
            allowed.append(key_minus_query < jnp.where(j >= 0, (i - j) * T, SB_NEVER))
            for p in range(P):
                ks[p, b] = k_ref[pl.ds(r0, T), p * T:(p + 1) * T].astype(bf16)
                vs[p, b] = v_ref[pl.ds(r0, T), p * T:(p + 1) * T].astype(bf16)
        zs = {c: _nt(qms[c[0]], ks[c]) for c in chains}
        lss, his, los = {}, {}, {}
        for c in chains:
            z = zs[c]
            lss[c] = jnp.minimum(z, 0.0) - jnp.log(1.0 + jnp.exp(-jnp.abs(z)))
            his[c], los[c] = _split_bf16(jnp.where(allowed[c[1]], lss[c] - z, 0.0))
        rs = {c: _dot(his[c], u) + _dot(los[c], u) for c in chains}
        ws = {}
        for p, b in chains:
            r = rs[p, b]
            ws[p, b] = jnp.where(allowed[b], jnp.exp(lss[p, b] + r[:, :T] + carries[p]), 0.0).astype(bf16)
            carries[p] = carries[p] + r[:, T:]
        for p, b in chains:
            accs[p] = accs[p] + _dot(ws[p, b], vs[p, b])
        return j_hi - SB_GROUP, tuple(carries), tuple(accs)

    def live(state):
        j_hi, carries, _ = state
        top = carries[0]
        for c in carries[1:]:
            top = jnp.maximum(top, c)
        return (j_hi >= 0) & (jnp.max(top) > SB_DEAD)

    zero = jnp.zeros((2 * T, T), f32)
    _, _, accs = lax.while_loop(live, trip, (i, (zero,) * P, (zero,) * P))
    for p in range(P):
        o_ref[:, p * T:(p + 1) * T] = jnp.where(lane < SB_DH, accs[p][:T], accs[p][T:])


def _sb_call(proj, B, S):
    N = proj.shape[0]
    nq = S // SB_BLOCK
    W = SB_PAIRS * LANES
    return pl.pallas_call(
        _sb_kernel,
        grid=(B, SB_HEADS // (2 * SB_PAIRS), nq),
        in_specs=[pl.BlockSpec((SB_BLOCK, W), lambda b, p, i: (b * nq + i, COL_QB // SB_PAIRS + p)),
                  pl.BlockSpec((S, W), lambda b, p, i: (b, COL_KB // SB_PAIRS + p)),
                  pl.BlockSpec((S, W), lambda b, p, i: (b, COL_VB // SB_PAIRS + p))],
        out_specs=pl.BlockSpec((SB_BLOCK, W), lambda b, p, i: (b * nq + i, p)),
        out_shape=jax.ShapeDtypeStruct((N, SB_W), f32),
        compiler_params=_params(("arbitrary", "arbitrary", "arbitrary")),
        name="sb",
    )(proj, proj, proj)


def _layer_norm(r, g, b):
    mu = jnp.mean(r, axis=-1, keepdims=True)
    d = r - mu
    var = jnp.mean(d * d, axis=-1, keepdims=True)
    return d * lax.rsqrt(var + LN_EPS) * g + b


def _merge_kernel(x_ref, oa_ref, ga_ref, ob_ref, gta0_ref, gta1_ref, gtb0_ref, gtb1_ref,
                  gt1_ref, sh2_ref, sc2_ref, hgn_ref, wua_ref, wub_ref, wo_ref, wpq_ref,
                  lng_ref, lnb_ref, x1_ref, hf_ref, pq_ref):
    oa = oa_ref[...]
    hgn = hgn_ref[...]
    segs = []
    for h in range(HG_HEADS):
        seg = oa[:, h * HG_D:(h + 1) * HG_D]
        ms = jnp.mean(seg * seg, axis=-1, keepdims=True)
        segs.append(seg * lax.rsqrt(ms + RMS_EPS) * hgn[:, h * HG_D:(h + 1) * HG_D])
    ga = ga_ref[...]
    oa_n = jnp.concatenate(segs, axis=-1) * (ga * jax.nn.sigmoid(ga))
    ma = _dot(oa_n.astype(bf16), wua_ref[...])
    mb = _dot(ob_ref[...].astype(bf16), wub_ref[...])
    gate_a = jnp.concatenate([gta0_ref[...], gta1_ref[...]], axis=-1)
    gate_b = jnp.concatenate([gtb0_ref[...], gtb1_ref[...]], axis=-1)
    merged = jax.nn.sigmoid(gate_a) * ma + jax.nn.sigmoid(gate_b) * mb
    y = _dot(merged.astype(bf16), wo_ref[...]) * gt1_ref[...]
    x1 = _layer_norm(DN_ALPHA * x_ref[...] + y, lng_ref[...], lnb_ref[...])
    x1_ref[...] = x1
    hf = x1 * (1.0 + sc2_ref[...]) + sh2_ref[...]
    tm = hf.shape[0]
    for r in range(SUBLANES):
        hf_ref[pl.ds(r, tm, stride=SUBLANES), :] = hf[:, r * LANES:(r + 1) * LANES]
    pq_ref[...] = _dot(hf.astype(bf16), wpq_ref[...])


def _merge_call(x, oa, proj, ob, mod6, hgn, wua, wub, wo, wpq, lng, lnb, S):
    N = x.shape[0]
    tm = 256
    tps = S // tm

    def full(shape):
        return pl.BlockSpec(shape, lambda i: (0,) * len(shape))

    def p512(c):
        return pl.BlockSpec((tm, 512), lambda i: (i, c))

    row = pl.BlockSpec((tm, D_MODEL), lambda i: (i, 0))
    out = jax.ShapeDtypeStruct((N, D_MODEL), f32)
    return pl.pallas_call(
        _merge_kernel,
        grid=(N // tm,),
        in_specs=[row, pl.BlockSpec((tm, HG_W), lambda i: (i, 0)), p512(COL512_GA),
                  pl.BlockSpec((tm, SB_W), lambda i: (i, 0)),
                  p512(COL512_GATE_A), p512(COL512_GATE_A + 1),
                  p512(COL512_GATE_B), p512(COL512_GATE_B + 1),
                  _mod_spec(2, tps), _mod_spec(3, tps), _mod_spec(4, tps),
                  full((1, HG_W)), full((HG_W, D_MODEL)), full((SB_W, D_MODEL)),
                  full((D_MODEL, D_MODEL)), full((D_MODEL, D_MODEL)),
                  full((1, D_MODEL)), full((1, D_MODEL))],
        out_specs=[row, pl.BlockSpec((tm * SUBLANES, LANES), lambda i: (i, 0)), row],
        out_shape=[out, jax.ShapeDtypeStruct((N * SUBLANES, LANES), f32), out],
        compiler_params=_params(("arbitrary",)),
        name="merge",
    )(x, oa, proj, ob, proj, proj, proj, proj, mod6, mod6, mod6, hgn, wua, wub, wo, wpq, lng, lnb)


TOPK_TOKENS = 256


def _top16(s, payload=None):
    R = s.shape[0]
    rid = lax.broadcasted_iota(i32, s.shape, 0).astype(f32)
    vals, ids = [], []
    for _ in range(PEER_TOPK):
        m = jnp.max(s, axis=0, keepdims=True)
        first = jnp.min(jnp.where(s == m, rid, float(R)), axis=0, keepdims=True)
        sel = rid == first
        vals.append(m)
        if payload is None:
            ids.append(first.astype(i32))
        else:
            ids.append(jnp.max(jnp.where(sel, payload, -1), axis=0, keepdims=True))
        s = jnp.where(sel, -jnp.inf, s)
    return vals, ids


def _topk_kernel(pq_ref, k1_ref, k2_ref, after_ref, idx_ref, gate_ref, idx_t, gate_t):
    del after_ref
    H = SUBLANES

    def halves(h):
        c0 = pl.multiple_of(h * PEER_DKEY, PEER_DKEY)
        qh = pq_ref[:, pl.ds(c0, PEER_DKEY)].astype(bf16)
        v1, i1 = _top16(_nt(k1_ref[h], qh))
        v2, i2 = _top16(_nt(k2_ref[h], qh))
        return tuple(jnp.concatenate(a, axis=0) for a in (v1, i1, v2, i2))

    def combine(h, tops):
        v1a, i1a, v2a, i2a = tops
        v1 = [v1a[a:a + 1] for a in range(H)]
        i1 = [i1a[a:a + 1] for a in range(H)]
        cand = ([v1[0] + v2a[:H], v1[0] + v2a[H:]] + [v1[a] + v2a[:H] for a in range(1, H)]
                + [v1a[H:] + v2a[0:1]])
        cidx = ([i1[0] * PEER_NKEYS + i2a[:H], i1[0] * PEER_NKEYS + i2a[H:]]
                + [i1[a] * PEER_NKEYS + i2a[:H] for a in range(1, H)] + [i1a[H:] * PEER_NKEYS + i2a[0:1]])
        tv, ti = _top16(jnp.concatenate(cand, axis=0), jnp.concatenate(cidx, axis=0) * ROW_SUB)
        tva = jnp.concatenate(tv, axis=0)
        e = jnp.exp(tva - tv[0])
        r0 = pl.multiple_of(h * PEER_TOPK, PEER_TOPK)
        gate_t[pl.ds(r0, PEER_TOPK), :] = e / jnp.sum(e, axis=0, keepdims=True)
        idx_t[pl.ds(r0, PEER_TOPK), :] = jnp.concatenate(ti, axis=0)

    def step(h, tops):
        nxt = halves(h + 1)
        combine(h, tops)
        return nxt

    last = lax.fori_loop(0, PEER_HEADS - 1, step, halves(0))
    combine(PEER_HEADS - 1, last)
    idx_ref[...] = idx_t[...].T
    gate_ref[...] = gate_t[...].T


def _topk_call(pq, k1p, k2p, start, count, after):
    T = TOPK_TOKENS
    first = start // T
    keys = pl.BlockSpec((PEER_HEADS, PEER_NKEYS, PEER_DKEY), lambda i: (0, 0, 0))
    return pl.pallas_call(
        _topk_kernel,
        grid=(count // T,),
        in_specs=[pl.BlockSpec((T, D_MODEL), lambda i: (i + first, 0)), keys, keys,
                  pl.BlockSpec(memory_space=pl.ANY)],
        out_specs=[pl.BlockSpec((T, PEER_PAIRS), lambda i: (i, 0)),
                   pl.BlockSpec((T, PEER_PAIRS), lambda i: (i, 0))],
        out_shape=[jax.ShapeDtypeStruct((count, PEER_PAIRS), i32),
                   jax.ShapeDtypeStruct((count, PEER_PAIRS), f32)],
        scratch_shapes=[pltpu.VMEM((PEER_PAIRS, T), i32), pltpu.VMEM((PEER_PAIRS, T), f32)],
        compiler_params=_params(("arbitrary",)),
        name="topk",
    )(pq, k1p, k2p, after)


PEER_TOKENS = 64
PEER_GROUP = SUBLANES
STAGE_ROWS = PEER_PAIRS * ROW_SUB
STAGE_COLS = 2 * STAGE_ROWS
INV_SQRT2 = 1.0 / math.sqrt(2.0)


def _gather_rows(idx_s, k0, tab_ref, stage_ref):
    for k in range(PEER_PAIRS):
        e4 = pl.multiple_of(idx_s[k0 + k], ROW_SUB)
        stage_ref[k * ROW_SUB:(k + 1) * ROW_SUB, :] = tab_ref[pl.ds(e4, ROW_SUB), :]


def _piece_diag():
    sub = lax.broadcasted_iota(i32, (SUBLANES, STAGE_COLS), 0)
    col = lax.broadcasted_iota(i32, (SUBLANES, STAGE_COLS), 1)
    return (col & (SUBLANES - 1)) == 2 * (sub & (ROW_SUB - 1)) + lax.shift_right_logical(sub, 2)


def _peer_u_kernel(idx_s, hf_ref, g_ref, fold_ref, tab_ref, o_ref, stage, zs_ref):
    T = o_ref.shape[0]
    G = PEER_GROUP
    diag = _piece_diag()

    def group(gi, _):
        g0 = pl.multiple_of(gi * G, G)
        for t in range(G):
            _gather_rows(idx_s, (g0 + t) * PEER_PAIRS, tab_ref, stage)
            h_hi, h_lo = _split_bf16(hf_ref[pl.ds(pl.multiple_of((g0 + t) * SUBLANES, SUBLANES), SUBLANES), :])
            z = _nt(jnp.concatenate([h_hi, h_lo], axis=0), pltpu.bitcast(stage[...], bf16))
            zs_ref[t:t + 1, :] = jnp.sum(jnp.where(diag, z[:SUBLANES] + z[SUBLANES:], 0.0), axis=0, keepdims=True)
        zs_hi, zs_lo = _split_bf16(zs_ref[...])
        dots = _dot(zs_hi, fold_ref[...]) + _dot(zs_lo, fold_ref[...])
        act = 0.5 * dots * (1.0 + lax.erf(dots * INV_SQRT2))
        o_ref[pl.ds(g0, G), :] = act * g_ref[pl.ds(g0, G), :]
        return 0

    lax.fori_loop(0, T // G, group, 0)


def _peer_v_kernel(idx_s, w_ref, spread_ref, tab_ref, o_ref, stage):
    T = w_ref.shape[0]
    G = PEER_GROUP
    diag = _piece_diag()

    def group(gi, _):
        g0 = pl.multiple_of(gi * G, G)
        w8 = _dot(w_ref[pl.ds(g0, G), :].astype(bf16), spread_ref[...])
        for t in range(G):
            _gather_rows(idx_s, (g0 + t) * PEER_PAIRS, tab_ref, stage)
            wexp = jnp.where(diag, jnp.broadcast_to(w8[t:t + 1, :], diag.shape), 0.0).astype(bf16)
            r0 = pl.multiple_of((g0 + t) * SUBLANES, SUBLANES)
            o_ref[pl.ds(r0, SUBLANES), :] = _dot(wexp, pltpu.bitcast(stage[...], bf16))
        return 0

    lax.fori_loop(0, T // G, group, 0)


def _table_spec():
    return pl.BlockSpec((PEER_N * ROW_SUB, LANES), lambda i: (0, 0), pipeline_mode=pl.Buffered(1))


def _smem_flat(T, first):
    return pl.BlockSpec((T * PEER_PAIRS,), lambda i: (i + first,), memory_space=pltpu.SMEM)


def _pair_pieces():
    k = lax.broadcasted_iota(i32, (PEER_PAIRS, STAGE_COLS), 0)
    c = lax.broadcasted_iota(i32, (PEER_PAIRS, STAGE_COLS), 1)
    return (c // SUBLANES == k).astype(bf16)


def _peer_u_call(idx4, hf8, gates, u_pk, local, start, count):
    T = PEER_TOKENS
    first = start // T
    first_local = local // T
    return pl.pallas_call(
        _peer_u_kernel,
        grid=(count // T,),
        in_specs=[_smem_flat(T, first_local),
                  pl.BlockSpec((T * SUBLANES, LANES), lambda i: (i + first, 0)),
                  pl.BlockSpec((T, PEER_PAIRS), lambda i: (i + first_local, 0)),
                  pl.BlockSpec((STAGE_COLS, PEER_PAIRS), lambda i: (0, 0)), _table_spec()],
        out_specs=pl.BlockSpec((T, PEER_PAIRS), lambda i: (i, 0)),
        out_shape=jax.ShapeDtypeStruct((count, PEER_PAIRS), f32),
        scratch_shapes=[pltpu.VMEM((STAGE_ROWS, LANES), i32), pltpu.VMEM((PEER_GROUP, STAGE_COLS), f32)],
        compiler_params=_params(("arbitrary",)),
        name="peer_u",
    )(idx4.reshape(-1), hf8, gates, _pair_pieces().T, u_pk)


def _peer_v_call(idx4, wgt, v_pk, start):
    count = wgt.shape[0]
    T = PEER_TOKENS
    return pl.pallas_call(
        _peer_v_kernel,
        grid=(count // T,),
        in_specs=[_smem_flat(T, start // T), pl.BlockSpec((T, PEER_PAIRS), lambda i: (i, 0)),
                  pl.BlockSpec((PEER_PAIRS, STAGE_COLS), lambda i: (0, 0)), _table_spec()],
        out_specs=pl.BlockSpec((T * SUBLANES, LANES), lambda i: (i, 0)),
        out_shape=jax.ShapeDtypeStruct((count * SUBLANES, LANES), f32),
        scratch_shapes=[pltpu.VMEM((STAGE_ROWS, LANES), i32)],
        compiler_params=_params(("arbitrary",)),
        name="peer_v",
    )(idx4.reshape(-1), wgt, _pair_pieces(), v_pk)


SC_CORES = 2
SC_SUBCORES = 16
SC_LANES = 16
SC_WORKERS = SC_CORES * SC_SUBCORES
SC_CHUNK = 16
SC_GROUP = 8
SC_PIECES = (18432, 13568, 15104)


def _sc_params():
    cp = pltpu.CompilerParams()
    if "needs_layout_passes" in pltpu.CompilerParams.__dataclass_fields__:
        cp = dataclasses.replace(cp, needs_layout_passes=False)
    return cp


def _peer_v_sc_call(idx, wgt, v3):
    n = idx.shape[0]
    per_worker = n // SC_WORKERS
    groups = LANES // SC_LANES
    n_chunks = PEER_PAIRS // SC_CHUNK
    mesh = plsc.VectorSubcoreMesh(core_axis_name="c", subcore_axis_name="s")
    rows_t = pltpu.VMEM((SC_CHUNK, SUBLANES, LANES), f32)

    @functools.partial(
        pl.kernel, mesh=mesh, compiler_params=_sc_params(),
        out_type=jax.ShapeDtypeStruct((n, SUBLANES, LANES), f32),
        scratch_types=[pltpu.VMEM((SC_GROUP, PEER_PAIRS), i32), pltpu.VMEM((SC_GROUP, PEER_PAIRS), f32),
                       rows_t, rows_t, pltpu.VMEM((SC_CHUNK, SC_LANES), f32),
                       pltpu.VMEM((SC_GROUP, SUBLANES, LANES), f32),
                       pltpu.SemaphoreType.DMA, pltpu.SemaphoreType.DMA])
    def sc_kernel(idx_hbm, w_hbm, tab_hbm, out_hbm, idx_v, w_v, rows_a, rows_b, wb_v, out_v, sem_a, sem_b):
        worker = lax.axis_index("s") * SC_CORES + lax.axis_index("c")
        zero = jnp.zeros((SC_LANES,), f32)
        bufs = ((rows_a, sem_a), (rows_b, sem_b))

        def gather(j, c):
            rows, sem = bufs[c % 2]
            return pltpu.make_async_copy(tab_hbm.at[idx_v.at[j, pl.ds(c * SC_CHUNK, SC_CHUNK)]], rows, sem)

        @pl.loop(0, per_worker // SC_GROUP)
        def _(g):
            t0 = worker * per_worker + g * SC_GROUP
            pltpu.sync_copy(idx_hbm.at[pl.ds(t0, SC_GROUP)], idx_v)
            pltpu.sync_copy(w_hbm.at[pl.ds(t0, SC_GROUP)], w_v)
            gather(0, 0).start()

            @pl.loop(0, SC_GROUP)
            def _(j):
                for c in range(n_chunks):
                    if c + 1 < n_chunks:
                        gather(j, c + 1).start()
                    else:
                        @pl.when(j + 1 < SC_GROUP)
                        def _():
                            gather(j + 1, 0).start()
                    gather(j, c).wait()
                    rows = bufs[c % 2][0]

                    @pl.loop(0, SC_CHUNK)
                    def _(r):
                        wb_v[r, :] = plsc.load_gather(
                            w_v, [jnp.full((SC_LANES,), 0, i32) + j, jnp.full((SC_LANES,), c * SC_CHUNK, i32) + r])

                    for s2 in range(SUBLANES // 2):
                        def where(a, s2=s2):
                            return 2 * s2 + a // groups, pl.ds((a % groups) * SC_LANES, SC_LANES)

                        def row_body(r, accs, rows=rows, where=where):
                            wv = wb_v[r, :]
                            return tuple(accs[a] + wv * rows[(r,) + where(a)] for a in range(2 * groups))

                        if c == 0:
                            init = (zero,) * (2 * groups)
                        else:
                            init = tuple(out_v[(j,) + where(a)] for a in range(2 * groups))
                        accs = lax.fori_loop(0, SC_CHUNK, row_body, init)
                        for a in range(2 * groups):
                            out_v[(j,) + where(a)] = accs[a]

            pltpu.sync_copy(out_v, out_hbm.at[pl.ds(t0, SC_GROUP)])

    return sc_kernel(idx, wgt, v3)


def _ln_kernel(x_ref, y_ref, gt_ref, g_ref, b_ref, o_ref):
    tm = x_ref.shape[0]
    y = jnp.concatenate([y_ref[pl.ds(r, tm, stride=SUBLANES), :] for r in range(SUBLANES)], axis=-1)
    o_ref[...] = _layer_norm(DN_ALPHA * x_ref[...] + y * gt_ref[...], g_ref[...], b_ref[...])


def _ln_call(x1, y8, mod6, lng, lnb, S):
    N = x1.shape[0]
    tm = 512
    row = pl.BlockSpec((tm, D_MODEL), lambda i: (i, 0))
    vec = pl.BlockSpec((1, D_MODEL), lambda i: (0, 0))
    return pl.pallas_call(
        _ln_kernel,
        grid=(N // tm,),
        in_specs=[row, pl.BlockSpec((tm * SUBLANES, LANES), lambda i: (i, 0)), _mod_spec(5, S // tm), vec, vec],
        out_specs=row,
        out_shape=jax.ShapeDtypeStruct((N, D_MODEL), f32),
        compiler_params=_params(("arbitrary",)),
        name="ln",
    )(x1, y8, mod6, lng, lnb)


def _pack_table(t):
    tb = lax.bitcast_convert_type(t.astype(bf16), jnp.uint16).astype(jnp.uint32)
    word = tb[:, :ROW_WORDS] | (tb[:, ROW_WORDS:] << 16)
    return lax.bitcast_convert_type(word, i32).reshape(PEER_N * ROW_SUB, LANES)


def _pad_keys(sub_keys_l):
    half = PEER_DKEY // 2
    z = jnp.zeros((PEER_HEADS, PEER_NKEYS, half), f32)
    k1 = jnp.concatenate([sub_keys_l[0], z], axis=-1).astype(bf16)
    k2 = jnp.concatenate([z, sub_keys_l[1]], axis=-1).astype(bf16)
    return k1, k2


def _layer(x, c, lb_logits, B, S, layer, p):
    (w_ada, b_ada, w_in, hgn, wua, wub, wo, wpq, sub_keys, pu, pv, ln_g, ln_b) = p
    mod6 = _mod_call(c, w_ada, b_ada.reshape(1, -1)).reshape(B, 6, 1, D_MODEL)
    proj = _proj_call(x, mod6, w_in.astype(bf16), S)
    oa = _hgrn_call(layer.reshape(1), proj, lb_logits, B, S)
    ob = _sb_call(proj, B, S)
    x1, hf8, pq = _merge_call(x, oa, proj, ob, mod6, hgn.reshape(1, -1), wua.astype(bf16),
                              wub.astype(bf16), wo.astype(bf16), wpq.astype(bf16),
                              ln_g[0:1], ln_b[0:1], S)
    k1p, k2p = _pad_keys(sub_keys)
    N = x.shape[0]
    u_pk = _pack_table(pu)
    v3 = pv.reshape(PEER_N, SUBLANES, LANES)
    n0 = SC_PIECES[0]
    idx_0, gates_0 = _topk_call(pq, k1p, k2p, 0, n0, k1p)
    wgt_0 = _peer_u_call(idx_0, hf8, gates_0, u_pk, 0, 0, n0)
    pieces = [_peer_v_sc_call(lax.shift_right_logical(idx_0, 2), wgt_0, v3).reshape(n0 * SUBLANES, LANES)]
    idx_r, gates_r = _topk_call(pq, k1p, k2p, n0, N - n0, wgt_0)
    local = 0
    for count in SC_PIECES[1:]:
        wgt = _peer_u_call(idx_r, hf8, gates_r, u_pk, local, n0 + local, count)
        ids = lax.shift_right_logical(idx_r[local:local + count], 2)
        pieces.append(_peer_v_sc_call(ids, wgt, v3).reshape(count * SUBLANES, LANES))
        local += count
    wgt = _peer_u_call(idx_r, hf8, gates_r, u_pk, local, n0 + local, N - n0 - local)
    pieces.append(_peer_v_call(idx_r, wgt, _pack_table(pv), local))
    y8 = jnp.concatenate(pieces, axis=0)
    return _ln_call(x1, y8, mod6, ln_g[1:2], ln_b[1:2], S)


def kernel(x, c, w_ada, b_ada, w_in, lb_logits, hg_norm_g, w_up_a, w_up_b, w_o, w_pq, sub_keys,
           peer_u, peer_v, ln_g, ln_b):
    B, S, _ = x.shape
    xs = (jnp.arange(DEPTH, dtype=i32), w_ada, b_ada, w_in, hg_norm_g, w_up_a, w_up_b, w_o, w_pq,
          sub_keys, peer_u, peer_v, ln_g, ln_b)

    def step(xc, per_layer):
        return _layer(xc, c, lb_logits, B, S, per_layer[0], per_layer[1:]), None

    out, _ = lax.scan(step, x.reshape(B * S, D_MODEL), xs)
    return out.reshape(B, S, D_MODEL)
```

```python
import dataclasses
import functools
import math

import jax
import jax.numpy as jnp
from jax import lax
from jax.experimental import pallas as pl
from jax.experimental.pallas import tpu as pltpu
from jax.experimental.pallas import tpu_sc as plsc

f32 = jnp.float32
bf16 = jnp.bfloat16
i32 = jnp.int32

D_MODEL = 1024
DEPTH = 4
HG_HEADS = 4
HG_D = 128
HG_W = HG_HEADS * HG_D
SB_HEADS = 8
SB_DH = 64
SB_W = SB_HEADS * SB_DH
SB_BLOCK = 128
PEER_HEADS = 8
PEER_NKEYS = 128
PEER_N = PEER_NKEYS * PEER_NKEYS
PEER_DKEY = 128
PEER_TOPK = 16
PEER_PAIRS = PEER_HEADS * PEER_TOPK
IN_WIDTH = 4 * HG_W + 3 * SB_W + 2 * D_MODEL
DN_ALPHA = (2.0 * DEPTH) ** 0.25
LN_EPS = 1e-5
RMS_EPS = 1e-6

LANES = 128
SUBLANES = 8
ROW_WORDS = D_MODEL // 2
ROW_SUB = ROW_WORDS // LANES
VMEM_LIMIT = 48 * 1024 * 1024

COL_QA, COL_FA, COL_IA = 0, 4, 8
COL_QB, COL_KB, COL_VB = 16, 20, 24
COL512_GA, COL512_GATE_A, COL512_GATE_B = 3, 7, 9


def _nt(a, b):
    return lax.dot_general(a, b, (((1,), (1,)), ((), ())), preferred_element_type=f32)


def _tn(a, b):
    return lax.dot_general(a, b, (((0,), (0,)), ((), ())), preferred_element_type=f32)


def _dot(a, b):
    return jnp.dot(a, b, preferred_element_type=f32)


def _split_bf16(a):
    hi = a.astype(bf16)
    lo = (a - hi.astype(f32)).astype(bf16)
    return hi, lo


def _params(sem):
    return pltpu.CompilerParams(dimension_semantics=sem, vmem_limit_bytes=VMEM_LIMIT)


def _mod_kernel(c_ref, w_ref, b_ref, o_ref):
    c = c_ref[...]
    cond = (c * jax.nn.sigmoid(c)).astype(bf16)
    o_ref[...] = _dot(cond, w_ref[...].astype(bf16)) + b_ref[...]


def _mod_call(c, w_ada_l, b_ada_l):
    B = c.shape[0]
    tn = 1536
    return pl.pallas_call(
        _mod_kernel,
        grid=(6 * D_MODEL // tn,),
        in_specs=[pl.BlockSpec((B, D_MODEL), lambda j: (0, 0)),
                  pl.BlockSpec((D_MODEL, tn), lambda j: (0, j)),
                  pl.BlockSpec((1, tn), lambda j: (0, j))],
        out_specs=pl.BlockSpec((B, tn), lambda j: (0, j)),
        out_shape=jax.ShapeDtypeStruct((B, 6 * D_MODEL), f32),
        compiler_params=_params(("arbitrary",)),
        name="mod",
    )(c, w_ada_l, b_ada_l)


def _mod_spec(which, tiles_per_seq):
    return pl.BlockSpec((None, None, 1, D_MODEL), lambda i: (i // tiles_per_seq, which, 0, 0))


def _proj_kernel(x_ref, sh_ref, sc_ref, w_ref, o_ref):
    hm = (x_ref[...] * (1.0 + sc_ref[...]) + sh_ref[...]).astype(bf16)
    for n0 in range(0, IN_WIDTH, 512):
        o_ref[:, n0:n0 + 512] = _dot(hm, w_ref[:, n0:n0 + 512])


def _proj_call(x, mod6, w_in_bf, S):
    N = x.shape[0]
    tm = 256
    tps = S // tm
    return pl.pallas_call(
        _proj_kernel,
        grid=(N // tm,),
        in_specs=[pl.BlockSpec((tm, D_MODEL), lambda i: (i, 0)),
                  _mod_spec(0, tps), _mod_spec(1, tps),
                  pl.BlockSpec((D_MODEL, IN_WIDTH), lambda i: (0, 0))],
        out_specs=pl.BlockSpec((tm, IN_WIDTH), lambda i: (i, 0)),
        out_shape=jax.ShapeDtypeStruct((N, IN_WIDTH), f32),
        compiler_params=_params(("arbitrary",)),
        name="proj",
    )(x, mod6, mod6, w_in_bf)


HG_CHUNK = 128
HG_SUB = 16
HG_STEP_ROWS = 512


def _hgrn_chunk(q, z, v, lb, st, tri):
    one_m_lb = 1.0 - lb
    g = jnp.log(lb + one_m_lb * jax.nn.sigmoid(z))
    k = one_m_lb * jax.nn.sigmoid(-z)
    g_hi, g_lo = _split_bf16(g)
    b = _dot(tri, g_hi) + _dot(tri, g_lo)
    b_end = b[HG_CHUNK - 1:HG_CHUNK, :]
    inter = _nt((q * jnp.exp(b)).astype(bf16), st.astype(bf16))
    v_bf = v.astype(bf16)
    row_c = lax.broadcasted_iota(i32, (HG_CHUNK, HG_D), 0)
    row_s = lax.broadcasted_iota(i32, (HG_SUB, HG_D), 0)
    blocks = []
    for sub in range(HG_CHUNK // HG_SUB):
        r0 = sub * HG_SUB
        bs = b[r0:r0 + HG_SUB]
        qs = q[r0:r0 + HG_SUB]
        ks = k[r0:r0 + HG_SUB]
        vs = v[r0:r0 + HG_SUB]
        rows = []
        for t in range(HG_SUB):
            m = row_s <= t
            e = jnp.where(m, jnp.exp(jnp.where(m, bs[t:t + 1] - bs, 0.0)), 0.0)
            p = (qs[t:t + 1] * ks) * e
            srow = jnp.sum(p, axis=-1, keepdims=True)
            rows.append(jnp.sum(srow * vs, axis=0, keepdims=True))
        o_sub = jnp.concatenate(rows, axis=0)
        if sub > 0:
            bref = b[r0 - 1:r0, :]
            qi = (qs * jnp.exp(bs - bref)).astype(bf16)
            past = row_c < r0
            ki = jnp.where(past, k * jnp.exp(jnp.where(past, bref - b, 0.0)), 0.0).astype(bf16)
            o_sub = o_sub + _dot(_nt(qi, ki).astype(bf16), v_bf)
        blocks.append(o_sub)
    intra = jnp.concatenate(blocks, axis=0)
    kd = (k * jnp.exp(b_end - b)).astype(bf16)
    st_new = st * jnp.exp(b_end) + _tn(v_bf, kd)
    return inter + intra, st_new


def _hgrn_kernel(l_ref, q_ref, z_ref, v_ref, lbl_ref, o_ref, st_ref):
    @pl.when(pl.program_id(2) == 0)
    def _():
        st_ref[...] = jnp.zeros_like(st_ref)

    logits = lbl_ref[...]
    e = jnp.exp(logits - jnp.max(logits, axis=0, keepdims=True))
    p = e / jnp.sum(e, axis=0, keepdims=True)
    rid = lax.broadcasted_iota(i32, p.shape, 0)
    l = l_ref[0]
    lb = jnp.sum(jnp.where((rid >= 1) & (rid <= l), p, 0.0), axis=0, keepdims=True)

    r = lax.broadcasted_iota(i32, (HG_CHUNK, HG_CHUNK), 0)
    c = lax.broadcasted_iota(i32, (HG_CHUNK, HG_CHUNK), 1)
    tri = jnp.where(c <= r, 1.0, 0.0).astype(bf16)

    def body(ci, st):
        r0 = pl.multiple_of(ci * HG_CHUNK, HG_CHUNK)
        out, st = _hgrn_chunk(q_ref[pl.ds(r0, HG_CHUNK), :], z_ref[pl.ds(r0, HG_CHUNK), :],
                              v_ref[pl.ds(r0, HG_CHUNK), :], lb, st, tri)
        o_ref[pl.ds(r0, HG_CHUNK), :] = out
        return st

    st_ref[...] = lax.fori_loop(0, HG_STEP_ROWS // HG_CHUNK, body, st_ref[...])


def _hgrn_call(layer, proj, lb_logits, B, S):
    N = proj.shape[0]
    R = HG_STEP_ROWS
    spb = S // R

    def col(c0):
        return pl.BlockSpec((R, HG_D), lambda b, h, s, l: (b * spb + s, c0 + h))

    grid_spec = pltpu.PrefetchScalarGridSpec(
        num_scalar_prefetch=1,
        grid=(B, HG_HEADS, spb),
        in_specs=[col(COL_QA), col(COL_FA), col(COL_IA),
                  pl.BlockSpec((DEPTH, HG_D), lambda b, h, s, l: (0, h))],
        out_specs=pl.BlockSpec((R, HG_D), lambda b, h, s, l: (b * spb + s, h)),
        scratch_shapes=[pltpu.VMEM((HG_D, HG_D), f32)],
    )
    return pl.pallas_call(
        _hgrn_kernel,
        grid_spec=grid_spec,
        out_shape=jax.ShapeDtypeStruct((N, HG_W), f32),
        compiler_params=_params(("arbitrary", "arbitrary", "arbitrary")),
        name="hgrn",
    )(layer, proj, proj, proj, lb_logits)


SB_SCALE = 1.0 / math.sqrt(SB_DH)
SB_PAIRS = 4
SB_GROUP = 3
SB_DEAD = -104.0
SB_NEVER = -(1 << 20)


def _sb_kernel(q_ref, k_ref, v_ref, o_ref):
    i = pl.program_id(2)
    T = SB_BLOCK
    P = SB_PAIRS
    lane = lax.broadcasted_iota(i32, (T, T), 1)
    lane2 = lax.broadcasted_iota(i32, (T, 2 * T), 1)
    row2 = lax.broadcasted_iota(i32, (T, 2 * T), 0)
    u = jnp.where((row2 > lane2) | (lane2 >= T), 1.0, 0.0).astype(bf16)
    qms = []
    for p in range(P):
        q2 = q_ref[:, p * T:(p + 1) * T] * SB_SCALE
        qms.append(jnp.concatenate([jnp.where(lane < SB_DH, q2, 0.0), jnp.where(lane >= SB_DH, q2, 0.0)],
                                   axis=0).astype(bf16))
    lane_s = lax.broadcasted_iota(i32, (2 * T, T), 1)
    row_s = lax.broadcasted_iota(i32, (2 * T, T), 0)
    key_minus_query = lane_s - (row_s & (T - 1))

    def trip(state):
        j_hi, carries, accs = state
        carries, accs = list(carries), list(accs)
        chains = [(p, b) for p in range(P) for b in range(SB_GROUP)]
        allowed, ks, vs = [], {}, {}
        for b in range(SB_GROUP):
            j = j_hi - b
            r0 = pl.multiple_of(jnp.maximum(j, 0) * T, T)
            allowed.append(key_minus_query < jnp.where(j >= 0, (i - j) * T, SB_NEVER))
            for p in range(P):
                ks[p, b] = k_ref[pl.ds(r0, T), p * T:(p + 1) * T].astype(bf16)
                vs[p, b] = v_ref[pl.ds(r0, T), p * T:(p + 1) * T].astype(bf16)
        zs = {c: _nt(qms[c[0]], ks[c]) for c in chains}
        lss, his, los = {}, {}, {}
        for c in chains:
            z = zs[c]
            lss[c] = jnp.minimum(z, 0.0) - jnp.log(1.0 + jnp.exp(-jnp.abs(z)))
            his[c], los[c] = _split_bf16(jnp.where(allowed[c[1]], lss[c] - z, 0.0))
        rs = {c: _dot(his[c], u) + _dot(los[c], u) for c in chains}
        ws = {}
        for p, b in chains:
            r = rs[p, b]
            ws[p, b] = jnp.where(allowed[b], jnp.exp(lss[p, b] + r[:, :T] + carries[p]), 0.0).astype(bf16)
            carries[p] = carries[p] + r[:, T:]
        for p, b in chains:
            accs[p] = accs[p] + _dot(ws[p, b], vs[p, b])
        return j_hi - SB_GROUP, tuple(carries), tuple(accs)

    def live(state):
        j_hi, carries, _ = state
        top = carries[0]
        for c in carries[1:]:
            top = jnp.maximum(top, c)
        return (j_hi >= 0) & (jnp.max(top) > SB_DEAD)

    zero = jnp.zeros((2 * T, T), f32)
    _, _, accs = lax.while_loop(live, trip, (i, (zero,) * P, (zero,) * P))
    for p in range(P):
        o_ref[:, p * T:(p + 1) * T] = jnp.where(lane < SB_DH, accs[p][:T], accs[p][T:])


def _sb_call(proj, B, S):
    N = proj.shape[0]
    nq = S // SB_BLOCK
    W = SB_PAIRS * LANES
    return pl.pallas_call(
        _sb_kernel,
        grid=(B, SB_HEADS // (2 * SB_PAIRS), nq),
        in_specs=[pl.BlockSpec((SB_BLOCK, W), lambda b, p, i: (b * nq + i, COL_QB // SB_PAIRS + p)),
                  pl.BlockSpec((S, W), lambda b, p, i: (b, COL_KB // SB_PAIRS + p)),
                  pl.BlockSpec((S, W), lambda b, p, i: (b, COL_VB // SB_PAIRS + p))],
        out_specs=pl.BlockSpec((SB_BLOCK, W), lambda b, p, i: (b * nq + i, p)),
        out_shape=jax.ShapeDtypeStruct((N, SB_W), f32),
        compiler_params=_params(("arbitrary", "arbitrary", "arbitrary")),
        name="sb",
    )(proj, proj, proj)


def _layer_norm(r, g, b):
    mu = jnp.mean(r, axis=-1, keepdims=True)
    d = r - mu
    var = jnp.mean(d * d, axis=-1, keepdims=True)
    return d * lax.rsqrt(var + LN_EPS) * g + b


def _merge_kernel(x_ref, oa_ref, ga_ref, ob_ref, gta0_ref, gta1_ref, gtb0_ref, gtb1_ref,
                  gt1_ref, sh2_ref, sc2_ref, hgn_ref, wua_ref, wub_ref, wo_ref, wpq_ref,
                  lng_ref, lnb_ref, x1_ref, hf_ref, pq_ref):
    oa = oa_ref[...]
    hgn = hgn_ref[...]
    segs = []
    for h in range(HG_HEADS):
        seg = oa[:, h * HG_D:(h + 1) * HG_D]
        ms = jnp.mean(seg * seg, axis=-1, keepdims=True)
        segs.append(seg * lax.rsqrt(ms + RMS_EPS) * hgn[:, h * HG_D:(h + 1) * HG_D])
    ga = ga_ref[...]
    oa_n = jnp.concatenate(segs, axis=-1) * (ga * jax.nn.sigmoid(ga))
    ma = _dot(oa_n.astype(bf16), wua_ref[...])
    mb = _dot(ob_ref[...].astype(bf16), wub_ref[...])
    gate_a = jnp.concatenate([gta0_ref[...], gta1_ref[...]], axis=-1)
    gate_b = jnp.concatenate([gtb0_ref[...], gtb1_ref[...]], axis=-1)
    merged = jax.nn.sigmoid(gate_a) * ma + jax.nn.sigmoid(gate_b) * mb
    y = _dot(merged.astype(bf16), wo_ref[...]) * gt1_ref[...]
    x1 = _layer_norm(DN_ALPHA * x_ref[...] + y, lng_ref[...], lnb_ref[...])
    x1_ref[...] = x1
    hf = x1 * (1.0 + sc2_ref[...]) + sh2_ref[...]
    tm = hf.shape[0]
    for r in range(SUBLANES):
        hf_ref[pl.ds(r, tm, stride=SUBLANES), :] = hf[:, r * LANES:(r + 1) * LANES]
    pq_ref[...] = _dot(hf.astype(bf16), wpq_ref[...])


def _merge_call(x, oa, proj, ob, mod6, hgn, wua, wub, wo, wpq, lng, lnb, S):
    N = x.shape[0]
    tm = 256
    tps = S // tm

    def full(shape):
        return pl.BlockSpec(shape, lambda i: (0,) * len(shape))

    def p512(c):
        return pl.BlockSpec((tm, 512), lambda i: (i, c))

    row = pl.BlockSpec((tm, D_MODEL), lambda i: (i, 0))
    out = jax.ShapeDtypeStruct((N, D_MODEL), f32)
    return pl.pallas_call(
        _merge_kernel,
        grid=(N // tm,),
        in_specs=[row, pl.BlockSpec((tm, HG_W), lambda i: (i, 0)), p512(COL512_GA),
                  pl.BlockSpec((tm, SB_W), lambda i: (i, 0)),
                  p512(COL512_GATE_A), p512(COL512_GATE_A + 1),
                  p512(COL512_GATE_B), p512(COL512_GATE_B + 1),
                  _mod_spec(2, tps), _mod_spec(3, tps), _mod_spec(4, tps),
                  full((1, HG_W)), full((HG_W, D_MODEL)), full((SB_W, D_MODEL)),
                  full((D_MODEL, D_MODEL)), full((D_MODEL, D_MODEL)),
                  full((1, D_MODEL)), full((1, D_MODEL))],
        out_specs=[row, pl.BlockSpec((tm * SUBLANES, LANES), lambda i: (i, 0)), row],
        out_shape=[out, jax.ShapeDtypeStruct((N * SUBLANES, LANES), f32), out],
        compiler_params=_params(("arbitrary",)),
        name="merge",
    )(x, oa, proj, ob, proj, proj, proj, proj, mod6, mod6, mod6, hgn, wua, wub, wo, wpq, lng, lnb)


TOPK_TOKENS = 256


def _top16(s, payload=None):
    R = s.shape[0]
    rid = lax.broadcasted_iota(i32, s.shape, 0).astype(f32)
    vals, ids = [], []
    for _ in range(PEER_TOPK):
        m = jnp.max(s, axis=0, keepdims=True)
        first = jnp.min(jnp.where(s == m, rid, float(R)), axis=0, keepdims=True)
        sel = rid == first
        vals.append(m)
        if payload is None:
            ids.append(first.astype(i32))
        else:
            ids.append(jnp.max(jnp.where(sel, payload, -1), axis=0, keepdims=True))
        s = jnp.where(sel, -jnp.inf, s)
    return vals, ids


def _topk_kernel(pq_ref, k1_ref, k2_ref, after_ref, idx_ref, gate_ref, idx_t, gate_t):
    del after_ref
    H = SUBLANES

    def halves(h):
        c0 = pl.multiple_of(h * PEER_DKEY, PEER_DKEY)
        qh = pq_ref[:, pl.ds(c0, PEER_DKEY)].astype(bf16)
        v1, i1 = _top16(_nt(k1_ref[h], qh))
        v2, i2 = _top16(_nt(k2_ref[h], qh))
        return tuple(jnp.concatenate(a, axis=0) for a in (v1, i1, v2, i2))

    def combine(h, tops):
        v1a, i1a, v2a, i2a = tops
        v1 = [v1a[a:a + 1] for a in range(H)]
        i1 = [i1a[a:a + 1] for a in range(H)]
        cand = ([v1[0] + v2a[:H], v1[0] + v2a[H:]] + [v1[a] + v2a[:H] for a in range(1, H)]
                + [v1a[H:] + v2a[0:1]])
        cidx = ([i1[0] * PEER_NKEYS + i2a[:H], i1[0] * PEER_NKEYS + i2a[H:]]
                + [i1[a] * PEER_NKEYS + i2a[:H] for a in range(1, H)] + [i1a[H:] * PEER_NKEYS + i2a[0:1]])
        tv, ti = _top16(jnp.concatenate(cand, axis=0), jnp.concatenate(cidx, axis=0) * ROW_SUB)
        tva = jnp.concatenate(tv, axis=0)
        e = jnp.exp(tva - tv[0])
        r0 = pl.multiple_of(h * PEER_TOPK, PEER_TOPK)
        gate_t[pl.ds(r0, PEER_TOPK), :] = e / jnp.sum(e, axis=0, keepdims=True)
        idx_t[pl.ds(r0, PEER_TOPK), :] = jnp.concatenate(ti, axis=0)

    def step(h, tops):
        nxt = halves(h + 1)
        combine(h, tops)
        return nxt

    last = lax.fori_loop(0, PEER_HEADS - 1, step, halves(0))
    combine(PEER_HEADS - 1, last)
    idx_ref[...] = idx_t[...].T
    gate_ref[...] = gate_t[...].T


def _topk_call(pq, k1p, k2p, start, count, after):
    T = TOPK_TOKENS
    first = start // T
    keys = pl.BlockSpec((PEER_HEADS, PEER_NKEYS, PEER_DKEY), lambda i: (0, 0, 0))
    return pl.pallas_call(
        _topk_kernel,
        grid=(count // T,),
        in_specs=[pl.BlockSpec((T, D_MODEL), lambda i: (i + first, 0)), keys, keys,
                  pl.BlockSpec(memory_space=pl.ANY)],
        out_specs=[pl.BlockSpec((T, PEER_PAIRS), lambda i: (i, 0)),
                   pl.BlockSpec((T, PEER_PAIRS), lambda i: (i, 0))],
        out_shape=[jax.ShapeDtypeStruct((count, PEER_PAIRS), i32),
                   jax.ShapeDtypeStruct((count, PEER_PAIRS), f32)],
        scratch_shapes=[pltpu.VMEM((PEER_PAIRS, T), i32), pltpu.VMEM((PEER_PAIRS, T), f32)],
        compiler_params=_params(("arbitrary",)),
        name="topk",
    )(pq, k1p, k2p, after)


PEER_TOKENS = 64
PEER_GROUP = SUBLANES
STAGE_ROWS = PEER_PAIRS * ROW_SUB
STAGE_COLS = 2 * STAGE_ROWS
INV_SQRT2 = 1.0 / math.sqrt(2.0)


def _gather_rows(idx_s, k0, tab_ref, stage_ref):
    for k in range(PEER_PAIRS):
        e4 = pl.multiple_of(idx_s[k0 + k], ROW_SUB)
        stage_ref[k * ROW_SUB:(k + 1) * ROW_SUB, :] = tab_ref[pl.ds(e4, ROW_SUB), :]


def _piece_diag():
    sub = lax.broadcasted_iota(i32, (SUBLANES, STAGE_COLS), 0)
    col = lax.broadcasted_iota(i32, (SUBLANES, STAGE_COLS), 1)
    return (col & (SUBLANES - 1)) == 2 * (sub & (ROW_SUB - 1)) + lax.shift_right_logical(sub, 2)


def _peer_u_kernel(idx_s, hf_ref, g_ref, fold_ref, tab_ref, o_ref, stage, zs_ref):
    T = o_ref.shape[0]
    G = PEER_GROUP
    diag = _piece_diag()

    def group(gi, _):
        g0 = pl.multiple_of(gi * G, G)
        for t in range(G):
            _gather_rows(idx_s, (g0 + t) * PEER_PAIRS, tab_ref, stage)
            h_hi, h_lo = _split_bf16(hf_ref[pl.ds(pl.multiple_of((g0 + t) * SUBLANES, SUBLANES), SUBLANES), :])
            z = _nt(jnp.concatenate([h_hi, h_lo], axis=0), pltpu.bitcast(stage[...], bf16))
            zs_ref[t:t + 1, :] = jnp.sum(jnp.where(diag, z[:SUBLANES] + z[SUBLANES:], 0.0), axis=0, keepdims=True)
        zs_hi, zs_lo = _split_bf16(zs_ref[...])
        dots = _dot(zs_hi, fold_ref[...]) + _dot(zs_lo, fold_ref[...])
        act = 0.5 * dots * (1.0 + lax.erf(dots * INV_SQRT2))
        o_ref[pl.ds(g0, G), :] = act * g_ref[pl.ds(g0, G), :]
        return 0

    lax.fori_loop(0, T // G, group, 0)


def _peer_v_kernel(idx_s, w_ref, spread_ref, tab_ref, o_ref, stage):
    T = w_ref.shape[0]
    G = PEER_GROUP
    diag = _piece_diag()

    def group(gi, _):
        g0 = pl.multiple_of(gi * G, G)
        w8 = _dot(w_ref[pl.ds(g0, G), :].astype(bf16), spread_ref[...])
        for t in range(G):
            _gather_rows(idx_s, (g0 + t) * PEER_PAIRS, tab_ref, stage)
            wexp = jnp.where(diag, jnp.broadcast_to(w8[t:t + 1, :], diag.shape), 0.0).astype(bf16)
            r0 = pl.multiple_of((g0 + t) * SUBLANES, SUBLANES)
            o_ref[pl.ds(r0, SUBLANES), :] = _dot(wexp, pltpu.bitcast(stage[...], bf16))
        return 0

    lax.fori_loop(0, T // G, group, 0)


def _table_spec():
    return pl.BlockSpec((PEER_N * ROW_SUB, LANES), lambda i: (0, 0), pipeline_mode=pl.Buffered(1))


def _smem_flat(T):
    return pl.BlockSpec((T * PEER_PAIRS,), lambda i: (i,), memory_space=pltpu.SMEM)


def _pair_pieces():
    k = lax.broadcasted_iota(i32, (PEER_PAIRS, STAGE_COLS), 0)
    c = lax.broadcasted_iota(i32, (PEER_PAIRS, STAGE_COLS), 1)
    return (c // SUBLANES == k).astype(bf16)


def _peer_u_call(idx4, hf8, gates, u_pk, start):
    count = idx4.shape[0]
    T = PEER_TOKENS
    first = start // T
    return pl.pallas_call(
        _peer_u_kernel,
        grid=(count // T,),
        in_specs=[_smem_flat(T),
                  pl.BlockSpec((T * SUBLANES, LANES), lambda i: (i + first, 0)),
                  pl.BlockSpec((T, PEER_PAIRS), lambda i: (i, 0)),
                  pl.BlockSpec((STAGE_COLS, PEER_PAIRS), lambda i: (0, 0)), _table_spec()],
        out_specs=pl.BlockSpec((T, PEER_PAIRS), lambda i: (i, 0)),
        out_shape=jax.ShapeDtypeStruct((count, PEER_PAIRS), f32),
        scratch_shapes=[pltpu.VMEM((STAGE_ROWS, LANES), i32), pltpu.VMEM((PEER_GROUP, STAGE_COLS), f32)],
        compiler_params=_params(("arbitrary",)),
        name="peer_u",
    )(idx4.reshape(-1), hf8, gates, _pair_pieces().T, u_pk)


def _peer_v_call(idx4, wgt, v_pk):
    count = wgt.shape[0]
    T = PEER_TOKENS
    return pl.pallas_call(
        _peer_v_kernel,
        grid=(count // T,),
        in_specs=[_smem_flat(T), pl.BlockSpec((T, PEER_PAIRS), lambda i: (i, 0)),
                  pl.BlockSpec((PEER_PAIRS, STAGE_COLS), lambda i: (0, 0)), _table_spec()],
        out_specs=pl.BlockSpec((T * SUBLANES, LANES), lambda i: (i, 0)),
        out_shape=jax.ShapeDtypeStruct((count * SUBLANES, LANES), f32),
        scratch_shapes=[pltpu.VMEM((STAGE_ROWS, LANES), i32)],
        compiler_params=_params(("arbitrary",)),
        name="peer_v",
    )(idx4.reshape(-1), wgt, _pair_pieces(), v_pk)


SC_CORES = 2
SC_SUBCORES = 16
SC_LANES = 16
SC_WORKERS = SC_CORES * SC_SUBCORES
SC_CHUNK = 16
SC_GROUP = 8
SC_PIECES = (7936, 9728, 11776, 12032, 11264)


def _sc_params():
    cp = pltpu.CompilerParams()
    if "needs_layout_passes" in pltpu.CompilerParams.__dataclass_fields__:
        cp = dataclasses.replace(cp, needs_layout_passes=False)
    return cp


def _peer_v_sc_call(idx, wgt, v3):
    n = idx.shape[0]
    per_worker = n // SC_WORKERS
    groups = LANES // SC_LANES
    n_chunks = PEER_PAIRS // SC_CHUNK
    mesh = plsc.VectorSubcoreMesh(core_axis_name="c", subcore_axis_name="s")
    rows_t = pltpu.VMEM((SC_CHUNK, SUBLANES, LANES), f32)

    @functools.partial(
        pl.kernel, mesh=mesh, compiler_params=_sc_params(),
        out_type=jax.ShapeDtypeStruct((n, SUBLANES, LANES), f32),
        scratch_types=[pltpu.VMEM((SC_GROUP, PEER_PAIRS), i32), pltpu.VMEM((SC_GROUP, PEER_PAIRS), f32),
                       rows_t, rows_t, pltpu.VMEM((SC_CHUNK, SC_LANES), f32),
                       pltpu.VMEM((SC_GROUP, SUBLANES, LANES), f32),
                       pltpu.SemaphoreType.DMA, pltpu.SemaphoreType.DMA])
    def sc_kernel(idx_hbm, w_hbm, tab_hbm, out_hbm, idx_v, w_v, rows_a, rows_b, wb_v, out_v, sem_a, sem_b):
        worker = lax.axis_index("s") * SC_CORES + lax.axis_index("c")
        zero = jnp.zeros((SC_LANES,), f32)
        bufs = ((rows_a, sem_a), (rows_b, sem_b))

        def gather(j, c):
            rows, sem = bufs[c % 2]
            return pltpu.make_async_copy(tab_hbm.at[idx_v.at[j, pl.ds(c * SC_CHUNK, SC_CHUNK)]], rows, sem)

        @pl.loop(0, per_worker // SC_GROUP)
        def _(g):
            t0 = worker * per_worker + g * SC_GROUP
            pltpu.sync_copy(idx_hbm.at[pl.ds(t0, SC_GROUP)], idx_v)
            pltpu.sync_copy(w_hbm.at[pl.ds(t0, SC_GROUP)], w_v)
            gather(0, 0).start()

            @pl.loop(0, SC_GROUP)
            def _(j):
                for c in range(n_chunks):
                    if c + 1 < n_chunks:
                        gather(j, c + 1).start()
                    else:
                        @pl.when(j + 1 < SC_GROUP)
                        def _():
                            gather(j + 1, 0).start()
                    gather(j, c).wait()
                    rows = bufs[c % 2][0]

                    @pl.loop(0, SC_CHUNK)
                    def _(r):
                        wb_v[r, :] = plsc.load_gather(
                            w_v, [jnp.full((SC_LANES,), 0, i32) + j, jnp.full((SC_LANES,), c * SC_CHUNK, i32) + r])

                    for s2 in range(SUBLANES // 2):
                        def where(a, s2=s2):
                            return 2 * s2 + a // groups, pl.ds((a % groups) * SC_LANES, SC_LANES)

                        def row_body(r, accs, rows=rows, where=where):
                            wv = wb_v[r, :]
                            return tuple(accs[a] + wv * rows[(r,) + where(a)] for a in range(2 * groups))

                        if c == 0:
                            init = (zero,) * (2 * groups)
                        else:
                            init = tuple(out_v[(j,) + where(a)] for a in range(2 * groups))
                        accs = lax.fori_loop(0, SC_CHUNK, row_body, init)
                        for a in range(2 * groups):
                            out_v[(j,) + where(a)] = accs[a]

            pltpu.sync_copy(out_v, out_hbm.at[pl.ds(t0, SC_GROUP)])

    return sc_kernel(idx, wgt, v3)


def _ln_kernel(x_ref, y_ref, gt_ref, g_ref, b_ref, o_ref):
    tm = x_ref.shape[0]
    y = jnp.concatenate([y_ref[pl.ds(r, tm, stride=SUBLANES), :] for r in range(SUBLANES)], axis=-1)
    o_ref[...] = _layer_norm(DN_ALPHA * x_ref[...] + y * gt_ref[...], g_ref[...], b_ref[...])


def _ln_call(x1, y8, mod6, lng, lnb, S):
    N = x1.shape[0]
    tm = 512
    row = pl.BlockSpec((tm, D_MODEL), lambda i: (i, 0))
    vec = pl.BlockSpec((1, D_MODEL), lambda i: (0, 0))
    return pl.pallas_call(
        _ln_kernel,
        grid=(N // tm,),
        in_specs=[row, pl.BlockSpec((tm * SUBLANES, LANES), lambda i: (i, 0)), _mod_spec(5, S // tm), vec, vec],
        out_specs=row,
        out_shape=jax.ShapeDtypeStruct((N, D_MODEL), f32),
        compiler_params=_params(("arbitrary",)),
        name="ln",
    )(x1, y8, mod6, lng, lnb)


def _pack_table(t):
    tb = lax.bitcast_convert_type(t.astype(bf16), jnp.uint16).astype(jnp.uint32)
    word = tb[:, :ROW_WORDS] | (tb[:, ROW_WORDS:] << 16)
    return lax.bitcast_convert_type(word, i32).reshape(PEER_N * ROW_SUB, LANES)


def _pad_keys(sub_keys_l):
    half = PEER_DKEY // 2
    z = jnp.zeros((PEER_HEADS, PEER_NKEYS, half), f32)
    k1 = jnp.concatenate([sub_keys_l[0], z], axis=-1).astype(bf16)
    k2 = jnp.concatenate([z, sub_keys_l[1]], axis=-1).astype(bf16)
    return k1, k2


def _layer(x, c, lb_logits, B, S, layer, p):
    (w_ada, b_ada, w_in, hgn, wua, wub, wo, wpq, sub_keys, pu, pv, ln_g, ln_b) = p
    mod6 = _mod_call(c, w_ada, b_ada.reshape(1, -1)).reshape(B, 6, 1, D_MODEL)
    proj = _proj_call(x, mod6, w_in.astype(bf16), S)
    oa = _hgrn_call(layer.reshape(1), proj, lb_logits, B, S)
    ob = _sb_call(proj, B, S)
    x1, hf8, pq = _merge_call(x, oa, proj, ob, mod6, hgn.reshape(1, -1), wua.astype(bf16),
                              wub.astype(bf16), wo.astype(bf16), wpq.astype(bf16),
                              ln_g[0:1], ln_b[0:1], S)
    k1p, k2p = _pad_keys(sub_keys)
    N = x.shape[0]
    u_pk = _pack_table(pu)
    v3 = pv.reshape(PEER_N, SUBLANES, LANES)
    pieces, start, after = [], 0, k1p
    for count in SC_PIECES + (N - sum(SC_PIECES),):
        idx_p, gates_p = _topk_call(pq, k1p, k2p, start, count, after)
        wgt = _peer_u_call(idx_p, hf8, gates_p, u_pk, start)
        if len(pieces) < len(SC_PIECES):
            y_p = _peer_v_sc_call(lax.shift_right_logical(idx_p, 2), wgt, v3).reshape(count * SUBLANES, LANES)
        else:
            y_p = _peer_v_call(idx_p, wgt, _pack_table(pv))
        pieces.append(y_p)
        start += count
        after = wgt
    y8 = jnp.concatenate(pieces, axis=0)
    return _ln_call(x1, y8, mod6, ln_g[1:2], ln_b[1:2], S)


def kernel(x, c, w_ada, b_ada, w_in, lb_logits, hg_norm_g, w_up_a, w_up_b, w_o, w_pq, sub_keys,
           peer_u, peer_v, ln_g, ln_b):
    B, S, _ = x.shape
    xs = (jnp.arange(DEPTH, dtype=i32), w_ada, b_ada, w_in, hg_norm_g, w_up_a, w_up_b, w_o, w_pq,
          sub_keys, peer_u, peer_v, ln_g, ln_b)

    def step(xc, per_layer):
        return _layer(xc, c, lb_logits, B, S, per_layer[0], per_layer[1:]), None

    out, _ = lax.scan(step, x.reshape(B * S, D_MODEL), xs)
    return out.reshape(B, S, D_MODEL)
```

```python
import dataclasses
import functools
import math

import jax
import jax.numpy as jnp
from jax import lax
from jax.experimental import pallas as pl
from jax.experimental.pallas import tpu as pltpu
from jax.experimental.pallas import tpu_sc as plsc

f32 = jnp.float32
bf16 = jnp.bfloat16
i32 = jnp.int32

D_MODEL = 1024
DEPTH = 4
HG_HEADS = 4
HG_D = 128
HG_W = HG_HEADS * HG_D
SB_HEADS = 8
SB_DH = 64
SB_W = SB_HEADS * SB_DH
SB_BLOCK = 128
PEER_HEADS = 8
PEER_NKEYS = 128
PEER_N = PEER_NKEYS * PEER_NKEYS
PEER_DKEY = 128
PEER_TOPK = 16
PEER_PAIRS = PEER_HEADS * PEER_TOPK
IN_WIDTH = 4 * HG_W + 3 * SB_W + 2 * D_MODEL
DN_ALPHA = (2.0 * DEPTH) ** 0.25
LN_EPS = 1e-5
RMS_EPS = 1e-6

LANES = 128
SUBLANES = 8
ROW_WORDS = D_MODEL // 2
ROW_SUB = ROW_WORDS // LANES
VMEM_LIMIT = 48 * 1024 * 1024

COL_QA, COL_FA, COL_IA = 0, 4, 8
COL_QB, COL_KB, COL_VB = 16, 20, 24
COL512_GA, COL512_GATE_A, COL512_GATE_B = 3, 7, 9


def _nt(a, b):
    return lax.dot_general(a, b, (((1,), (1,)), ((), ())), preferred_element_type=f32)


def _tn(a, b):
    return lax.dot_general(a, b, (((0,), (0,)), ((), ())), preferred_element_type=f32)


def _dot(a, b):
    return jnp.dot(a, b, preferred_element_type=f32)


def _split_bf16(a):
    hi = a.astype(bf16)
    lo = (a - hi.astype(f32)).astype(bf16)
    return hi, lo


def _params(sem):
    return pltpu.CompilerParams(dimension_semantics=sem, vmem_limit_bytes=VMEM_LIMIT)


def _mod_kernel(c_ref, w_ref, b_ref, o_ref):
    c = c_ref[...]
    cond = (c * jax.nn.sigmoid(c)).astype(bf16)
    o_ref[...] = _dot(cond, w_ref[...].astype(bf16)) + b_ref[...]


def _mod_call(c, w_ada_l, b_ada_l):
    B = c.shape[0]
    tn = 1536
    return pl.pallas_call(
        _mod_kernel,
        grid=(6 * D_MODEL // tn,),
        in_specs=[pl.BlockSpec((B, D_MODEL), lambda j: (0, 0)),
                  pl.BlockSpec((D_MODEL, tn), lambda j: (0, j)),
                  pl.BlockSpec((1, tn), lambda j: (0, j))],
        out_specs=pl.BlockSpec((B, tn), lambda j: (0, j)),
        out_shape=jax.ShapeDtypeStruct((B, 6 * D_MODEL), f32),
        compiler_params=_params(("arbitrary",)),
        name="mod",
    )(c, w_ada_l, b_ada_l)


def _mod_spec(which, tiles_per_seq):
    return pl.BlockSpec((None, None, 1, D_MODEL), lambda i: (i // tiles_per_seq, which, 0, 0))


def _proj_kernel(x_ref, sh_ref, sc_ref, w_ref, o_ref):
    hm = (x_ref[...] * (1.0 + sc_ref[...]) + sh_ref[...]).astype(bf16)
    for n0 in range(0, IN_WIDTH, 512):
        o_ref[:, n0:n0 + 512] = _dot(hm, w_ref[:, n0:n0 + 512])


def _proj_call(x, mod6, w_in_bf, S):
    N = x.shape[0]
    tm = 256
    tps = S // tm
    return pl.pallas_call(
        _proj_kernel,
        grid=(N // tm,),
        in_specs=[pl.BlockSpec((tm, D_MODEL), lambda i: (i, 0)),
                  _mod_spec(0, tps), _mod_spec(1, tps),
                  pl.BlockSpec((D_MODEL, IN_WIDTH), lambda i: (0, 0))],
        out_specs=pl.BlockSpec((tm, IN_WIDTH), lambda i: (i, 0)),
        out_shape=jax.ShapeDtypeStruct((N, IN_WIDTH), f32),
        compiler_params=_params(("arbitrary",)),
        name="proj",
    )(x, mod6, mod6, w_in_bf)


HG_CHUNK = 128
HG_SUB = 16
HG_STEP_ROWS = 512


def _hgrn_chunk(q, z, v, lb, st, tri):
    one_m_lb = 1.0 - lb
    g = jnp.log(lb + one_m_lb * jax.nn.sigmoid(z))
    k = one_m_lb * jax.nn.sigmoid(-z)
    g_hi, g_lo = _split_bf16(g)
    b = _dot(tri, g_hi) + _dot(tri, g_lo)
    b_end = b[HG_CHUNK - 1:HG_CHUNK, :]
    inter = _nt((q * jnp.exp(b)).astype(bf16), st.astype(bf16))
    v_bf = v.astype(bf16)
    row_c = lax.broadcasted_iota(i32, (HG_CHUNK, HG_D), 0)
    row_s = lax.broadcasted_iota(i32, (HG_SUB, HG_D), 0)
    blocks = []
    for sub in range(HG_CHUNK // HG_SUB):
        r0 = sub * HG_SUB
        bs = b[r0:r0 + HG_SUB]
        qs = q[r0:r0 + HG_SUB]
        ks = k[r0:r0 + HG_SUB]
        vs = v[r0:r0 + HG_SUB]
        rows = []
        for t in range(HG_SUB):
            m = row_s <= t
            e = jnp.where(m, jnp.exp(jnp.where(m, bs[t:t + 1] - bs, 0.0)), 0.0)
            p = (qs[t:t + 1] * ks) * e
            srow = jnp.sum(p, axis=-1, keepdims=True)
            rows.append(jnp.sum(srow * vs, axis=0, keepdims=True))
        o_sub = jnp.concatenate(rows, axis=0)
        if sub > 0:
            bref = b[r0 - 1:r0, :]
            qi = (qs * jnp.exp(bs - bref)).astype(bf16)
            past = row_c < r0
            ki = jnp.where(past, k * jnp.exp(jnp.where(past, bref - b, 0.0)), 0.0).astype(bf16)
            o_sub = o_sub + _dot(_nt(qi, ki).astype(bf16), v_bf)
        blocks.append(o_sub)
    intra = jnp.concatenate(blocks, axis=0)
    kd = (k * jnp.exp(b_end - b)).astype(bf16)
    st_new = st * jnp.exp(b_end) + _tn(v_bf, kd)
    return inter + intra, st_new


def _hgrn_kernel(l_ref, q_ref, z_ref, v_ref, lbl_ref, o_ref, st_ref):
    @pl.when(pl.program_id(2) == 0)
    def _():
        st_ref[...] = jnp.zeros_like(st_ref)

    logits = lbl_ref[...]
    e = jnp.exp(logits - jnp.max(logits, axis=0, keepdims=True))
    p = e / jnp.sum(e, axis=0, keepdims=True)
    rid = lax.broadcasted_iota(i32, p.shape, 0)
    l = l_ref[0]
    lb = jnp.sum(jnp.where((rid >= 1) & (rid <= l), p, 0.0), axis=0, keepdims=True)

    r = lax.broadcasted_iota(i32, (HG_CHUNK, HG_CHUNK), 0)
    c = lax.broadcasted_iota(i32, (HG_CHUNK, HG_CHUNK), 1)
    tri = jnp.where(c <= r, 1.0, 0.0).astype(bf16)

    def body(ci, st):
        r0 = pl.multiple_of(ci * HG_CHUNK, HG_CHUNK)
        out, st = _hgrn_chunk(q_ref[pl.ds(r0, HG_CHUNK), :], z_ref[pl.ds(r0, HG_CHUNK), :],
                              v_ref[pl.ds(r0, HG_CHUNK), :], lb, st, tri)
        o_ref[pl.ds(r0, HG_CHUNK), :] = out
        return st

    st_ref[...] = lax.fori_loop(0, HG_STEP_ROWS // HG_CHUNK, body, st_ref[...])


def _hgrn_call(layer, proj, lb_logits, B, S):
    N = proj.shape[0]
    R = HG_STEP_ROWS
    spb = S // R

    def col(c0):
        return pl.BlockSpec((R, HG_D), lambda b, h, s, l: (b * spb + s, c0 + h))

    grid_spec = pltpu.PrefetchScalarGridSpec(
        num_scalar_prefetch=1,
        grid=(B, HG_HEADS, spb),
        in_specs=[col(COL_QA), col(COL_FA), col(COL_IA),
                  pl.BlockSpec((DEPTH, HG_D), lambda b, h, s, l: (0, h))],
        out_specs=pl.BlockSpec((R, HG_D), lambda b, h, s, l: (b * spb + s, h)),
        scratch_shapes=[pltpu.VMEM((HG_D, HG_D), f32)],
    )
    return pl.pallas_call(
        _hgrn_kernel,
        grid_spec=grid_spec,
        out_shape=jax.ShapeDtypeStruct((N, HG_W), f32),
        compiler_params=_params(("arbitrary", "arbitrary", "arbitrary")),
        name="hgrn",
    )(layer, proj, proj, proj, lb_logits)


SB_SCALE = 1.0 / math.sqrt(SB_DH)
SB_PAIRS = 4
SB_GROUP = 3
SB_DEAD = -104.0
SB_NEVER = -(1 << 20)


def _sb_kernel(q_ref, k_ref, v_ref, o_ref):
    i = pl.program_id(2)
    T = SB_BLOCK
    P = SB_PAIRS
    lane = lax.broadcasted_iota(i32, (T, T), 1)
    lane2 = lax.broadcasted_iota(i32, (T, 2 * T), 1)
    row2 = lax.broadcasted_iota(i32, (T, 2 * T), 0)
    u = jnp.where((row2 > lane2) | (lane2 >= T), 1.0, 0.0).astype(bf16)
    qms = []
    for p in range(P):
        q2 = q_ref[:, p * T:(p + 1) * T] * SB_SCALE
        qms.append(jnp.concatenate([jnp.where(lane < SB_DH, q2, 0.0), jnp.where(lane >= SB_DH, q2, 0.0)],
                                   axis=0).astype(bf16))
    lane_s = lax.broadcasted_iota(i32, (2 * T, T), 1)
    row_s = lax.broadcasted_iota(i32, (2 * T, T), 0)
    key_minus_query = lane_s - (row_s & (T - 1))

    def trip(state):
        j_hi, carries, accs = state
        carries, accs = list(carries), list(accs)
        chains = [(p, b) for p in range(P) for b in range(SB_GROUP)]
        allowed, ks, vs = [], {}, {}
        for b in range(SB_GROUP):
            j = j_hi - b
            r0 = pl.multiple_of(jnp.maximum(j, 0) * T, T)
            allowed.append(key_minus_query < jnp.where(j >= 0, (i - j) * T, SB_NEVER))
            for p in range(P):
                ks[p, b] = k_ref[pl.ds(r0, T), p * T:(p + 1) * T].astype(bf16)
                vs[p, b] = v_ref[pl.ds(r0, T), p * T:(p + 1) * T].astype(bf16)
        zs = {c: _nt(qms[c[0]], ks[c]) for c in chains}
        lss, his, los = {}, {}, {}
        for c in chains:
            z = zs[c]
            lss[c] = jnp.minimum(z, 0.0) - jnp.log(1.0 + jnp.exp(-jnp.abs(z)))
            his[c], los[c] = _split_bf16(jnp.where(allowed[c[1]], lss[c] - z, 0.0))
        rs = {c: _dot(his[c], u) + _dot(los[c], u) for c in chains}
        ws = {}
        for p, b in chains:
            r = rs[p, b]
            ws[p, b] = jnp.where(allowed[b], jnp.exp(lss[p, b] + r[:, :T] + carries[p]), 0.0).astype(bf16)
            carries[p] = carries[p] + r[:, T:]
        for p, b in chains:
            accs[p] = accs[p] + _dot(ws[p, b], vs[p, b])
        return j_hi - SB_GROUP, tuple(carries), tuple(accs)

    def live(state):
        j_hi, carries, _ = state
        top = carries[0]
        for c in carries[1:]:
            top = jnp.maximum(top, c)
        return (j_hi >= 0) & (jnp.max(top) > SB_DEAD)

    zero = jnp.zeros((2 * T, T), f32)
    _, _, accs = lax.while_loop(live, trip, (i, (zero,) * P, (zero,) * P))
    for p in range(P):
        o_ref[:, p * T:(p + 1) * T] = jnp.where(lane < SB_DH, accs[p][:T], accs[p][T:])


def _sb_call(proj, B, S):
    N = proj.shape[0]
    nq = S // SB_BLOCK
    W = SB_PAIRS * LANES
    return pl.pallas_call(
        _sb_kernel,
        grid=(B, SB_HEADS // (2 * SB_PAIRS), nq),
        in_specs=[pl.BlockSpec((SB_BLOCK, W), lambda b, p, i: (b * nq + i, COL_QB // SB_PAIRS + p)),
                  pl.BlockSpec((S, W), lambda b, p, i: (b, COL_KB // SB_PAIRS + p)),
                  pl.BlockSpec((S, W), lambda b, p, i: (b, COL_VB // SB_PAIRS + p))],
        out_specs=pl.BlockSpec((SB_BLOCK, W), lambda b, p, i: (b * nq + i, p)),
        out_shape=jax.ShapeDtypeStruct((N, SB_W), f32),
        compiler_params=_params(("arbitrary", "arbitrary", "arbitrary")),
        name="sb",
    )(proj, proj, proj)


def _layer_norm(r, g, b):
    mu = jnp.mean(r, axis=-1, keepdims=True)
    d = r - mu
    var = jnp.mean(d * d, axis=-1, keepdims=True)
    return d * lax.rsqrt(var + LN_EPS) * g + b


def _merge_kernel(x_ref, oa_ref, ga_ref, ob_ref, gta0_ref, gta1_ref, gtb0_ref, gtb1_ref,
                  gt1_ref, sh2_ref, sc2_ref, hgn_ref, wua_ref, wub_ref, wo_ref, wpq_ref,
                  lng_ref, lnb_ref, x1_ref, hf_ref, pq_ref):
    oa = oa_ref[...]
    hgn = hgn_ref[...]
    segs = []
    for h in range(HG_HEADS):
        seg = oa[:, h * HG_D:(h + 1) * HG_D]
        ms = jnp.mean(seg * seg, axis=-1, keepdims=True)
        segs.append(seg * lax.rsqrt(ms + RMS_EPS) * hgn[:, h * HG_D:(h + 1) * HG_D])
    ga = ga_ref[...]
    oa_n = jnp.concatenate(segs, axis=-1) * (ga * jax.nn.sigmoid(ga))
    ma = _dot(oa_n.astype(bf16), wua_ref[...])
    mb = _dot(ob_ref[...].astype(bf16), wub_ref[...])
    gate_a = jnp.concatenate([gta0_ref[...], gta1_ref[...]], axis=-1)
    gate_b = jnp.concatenate([gtb0_ref[...], gtb1_ref[...]], axis=-1)
    merged = jax.nn.sigmoid(gate_a) * ma + jax.nn.sigmoid(gate_b) * mb
    y = _dot(merged.astype(bf16), wo_ref[...]) * gt1_ref[...]
    x1 = _layer_norm(DN_ALPHA * x_ref[...] + y, lng_ref[...], lnb_ref[...])
    x1_ref[...] = x1
    hf = x1 * (1.0 + sc2_ref[...]) + sh2_ref[...]
    tm = hf.shape[0]
    for r in range(SUBLANES):
        hf_ref[pl.ds(r, tm, stride=SUBLANES), :] = hf[:, r * LANES:(r + 1) * LANES]
    pq_ref[...] = _dot(hf.astype(bf16), wpq_ref[...])


def _merge_call(x, oa, proj, ob, mod6, hgn, wua, wub, wo, wpq, lng, lnb, S):
    N = x.shape[0]
    tm = 256
    tps = S // tm

    def full(shape):
        return pl.BlockSpec(shape, lambda i: (0,) * len(shape))

    def p512(c):
        return pl.BlockSpec((tm, 512), lambda i: (i, c))

    row = pl.BlockSpec((tm, D_MODEL), lambda i: (i, 0))
    out = jax.ShapeDtypeStruct((N, D_MODEL), f32)
    return pl.pallas_call(
        _merge_kernel,
        grid=(N // tm,),
        in_specs=[row, pl.BlockSpec((tm, HG_W), lambda i: (i, 0)), p512(COL512_GA),
                  pl.BlockSpec((tm, SB_W), lambda i: (i, 0)),
                  p512(COL512_GATE_A), p512(COL512_GATE_A + 1),
                  p512(COL512_GATE_B), p512(COL512_GATE_B + 1),
                  _mod_spec(2, tps), _mod_spec(3, tps), _mod_spec(4, tps),
                  full((1, HG_W)), full((HG_W, D_MODEL)), full((SB_W, D_MODEL)),
                  full((D_MODEL, D_MODEL)), full((D_MODEL, D_MODEL)),
                  full((1, D_MODEL)), full((1, D_MODEL))],
        out_specs=[row, pl.BlockSpec((tm * SUBLANES, LANES), lambda i: (i, 0)), row],
        out_shape=[out, jax.ShapeDtypeStruct((N * SUBLANES, LANES), f32), out],
        compiler_params=_params(("arbitrary",)),
        name="merge",
    )(x, oa, proj, ob, proj, proj, proj, proj, mod6, mod6, mod6, hgn, wua, wub, wo, wpq, lng, lnb)


TOPK_TOKENS = 256


def _top16(s, payload=None):
    R = s.shape[0]
    rid = lax.broadcasted_iota(i32, s.shape, 0).astype(f32)
    vals, ids = [], []
    for _ in range(PEER_TOPK):
        m = jnp.max(s, axis=0, keepdims=True)
        first = jnp.min(jnp.where(s == m, rid, float(R)), axis=0, keepdims=True)
        sel = rid == first
        vals.append(m)
        if payload is None:
            ids.append(first.astype(i32))
        else:
            ids.append(jnp.max(jnp.where(sel, payload, -1), axis=0, keepdims=True))
        s = jnp.where(sel, -jnp.inf, s)
    return vals, ids


def _topk_kernel(pq_ref, k1_ref, k2_ref, after_ref, idx_ref, gate_ref, idx_t, gate_t):
    del after_ref
    H = SUBLANES

    def halves(h):
        c0 = pl.multiple_of(h * PEER_DKEY, PEER_DKEY)
        qh = pq_ref[:, pl.ds(c0, PEER_DKEY)].astype(bf16)
        v1, i1 = _top16(_nt(k1_ref[h], qh))
        v2, i2 = _top16(_nt(k2_ref[h], qh))
        return tuple(jnp.concatenate(a, axis=0) for a in (v1, i1, v2, i2))

    def combine(h, tops):
        v1a, i1a, v2a, i2a = tops
        v1 = [v1a[a:a + 1] for a in range(H)]
        i1 = [i1a[a:a + 1] for a in range(H)]
        cand = ([v1[0] + v2a[:H], v1[0] + v2a[H:]] + [v1[a] + v2a[:H] for a in range(1, H)]
                + [v1a[H:] + v2a[0:1]])
        cidx = ([i1[0] * PEER_NKEYS + i2a[:H], i1[0] * PEER_NKEYS + i2a[H:]]
                + [i1[a] * PEER_NKEYS + i2a[:H] for a in range(1, H)] + [i1a[H:] * PEER_NKEYS + i2a[0:1]])
        tv, ti = _top16(jnp.concatenate(cand, axis=0), jnp.concatenate(cidx, axis=0) * ROW_SUB)
        tva = jnp.concatenate(tv, axis=0)
        e = jnp.exp(tva - tv[0])
        r0 = pl.multiple_of(h * PEER_TOPK, PEER_TOPK)
        gate_t[pl.ds(r0, PEER_TOPK), :] = e / jnp.sum(e, axis=0, keepdims=True)
        idx_t[pl.ds(r0, PEER_TOPK), :] = jnp.concatenate(ti, axis=0)

    def step(h, tops):
        nxt = halves(h + 1)
        combine(h, tops)
        return nxt

    last = lax.fori_loop(0, PEER_HEADS - 1, step, halves(0))
    combine(PEER_HEADS - 1, last)
    idx_ref[...] = idx_t[...].T
    gate_ref[...] = gate_t[...].T


def _topk_call(pq, k1p, k2p, start, count, after):
    T = TOPK_TOKENS
    first = start // T
    keys = pl.BlockSpec((PEER_HEADS, PEER_NKEYS, PEER_DKEY), lambda i: (0, 0, 0))
    return pl.pallas_call(
        _topk_kernel,
        grid=(count // T,),
        in_specs=[pl.BlockSpec((T, D_MODEL), lambda i: (i + first, 0)), keys, keys,
                  pl.BlockSpec(memory_space=pl.ANY)],
        out_specs=[pl.BlockSpec((T, PEER_PAIRS), lambda i: (i, 0)),
                   pl.BlockSpec((T, PEER_PAIRS), lambda i: (i, 0))],
        out_shape=[jax.ShapeDtypeStruct((count, PEER_PAIRS), i32),
                   jax.ShapeDtypeStruct((count, PEER_PAIRS), f32)],
        scratch_shapes=[pltpu.VMEM((PEER_PAIRS, T), i32), pltpu.VMEM((PEER_PAIRS, T), f32)],
        compiler_params=_params(("arbitrary",)),
        name="topk",
    )(pq, k1p, k2p, after)


PEER_TOKENS = 64
PEER_GROUP = SUBLANES
STAGE_ROWS = PEER_PAIRS * ROW_SUB
STAGE_COLS = 2 * STAGE_ROWS
INV_SQRT2 = 1.0 / math.sqrt(2.0)


def _gather_rows(idx_s, k0, tab_ref, stage_ref):
    for k in range(PEER_PAIRS):
        e4 = pl.multiple_of(idx_s[k0 + k], ROW_SUB)
        stage_ref[k * ROW_SUB:(k + 1) * ROW_SUB, :] = tab_ref[pl.ds(e4, ROW_SUB), :]


def _piece_diag():
    sub = lax.broadcasted_iota(i32, (SUBLANES, STAGE_COLS), 0)
    col = lax.broadcasted_iota(i32, (SUBLANES, STAGE_COLS), 1)
    return (col & (SUBLANES - 1)) == 2 * (sub & (ROW_SUB - 1)) + lax.shift_right_logical(sub, 2)


HI_MASK = -65536


def _peer_u_kernel(idx_s, hf_ref, g_ref, fold_ref, tab_ref, o_ref, stage, zs_ref, scr, a_all):
    T = o_ref.shape[0]
    G = PEER_GROUP
    diag = _piece_diag()
    ones = jnp.ones((SUBLANES, LANES), bf16)
    sub = lax.broadcasted_iota(i32, (G, PEER_PAIRS), 0)

    def group(gi, _):
        g0 = pl.multiple_of(gi * G, G)
        for t in range(G):
            k0 = (g0 + t) * PEER_PAIRS
            r0 = pl.multiple_of((g0 + t) * SUBLANES, SUBLANES)
            h = hf_ref[pl.ds(r0, SUBLANES), :]
            if t % 2 == 0:
                _gather_rows(idx_s, k0, tab_ref, stage)
                h_hi, h_lo = _split_bf16(h)
                z = _nt(jnp.concatenate([h_hi, h_lo], axis=0), pltpu.bitcast(stage[...], bf16))
                zs_ref[t // 2:t // 2 + 1, :] = jnp.sum(jnp.where(diag, z[:SUBLANES] + z[SUBLANES:], 0.0),
                                                       axis=0, keepdims=True)
            else:
                hlo, hhi = h[:ROW_SUB], h[ROW_SUB:]
                for k in range(PEER_PAIRS):
                    w = tab_ref[pl.ds(pl.multiple_of(idx_s[k0 + k], ROW_SUB), ROW_SUB), :]
                    lo = lax.bitcast_convert_type(lax.shift_left(w, 16), f32)
                    hi = lax.bitcast_convert_type(w & HI_MASK, f32)
                    scr[k * SUBLANES:k * SUBLANES + ROW_SUB, :] = lo * hlo + hi * hhi
                a = scr[pl.ds(0, PEER_PAIRS, stride=SUBLANES), :]
                for s_ in range(1, ROW_SUB):
                    a = a + scr[pl.ds(s_, PEER_PAIRS, stride=SUBLANES), :]
                a_all[(t // 2) * PEER_PAIRS:(t // 2 + 1) * PEER_PAIRS, :] = a
        zs_hi, zs_lo = _split_bf16(zs_ref[...])
        dots_m = _dot(zs_hi, fold_ref[...]) + _dot(zs_lo, fold_ref[...])
        a_hi, a_lo = _split_bf16(a_all[...])
        r = _nt(ones, a_hi) + _nt(ones, a_lo)
        dots = jnp.zeros((G, PEER_PAIRS), f32)
        for t in range(G):
            src = (dots_m[t // 2:t // 2 + 1, :] if t % 2 == 0
                   else r[0:1, (t // 2) * PEER_PAIRS:(t // 2 + 1) * PEER_PAIRS])
            dots = jnp.where(sub == t, jnp.broadcast_to(src, (G, PEER_PAIRS)), dots)
        act = 0.5 * dots * (1.0 + lax.erf(dots * INV_SQRT2))
        o_ref[pl.ds(g0, G), :] = act * g_ref[pl.ds(g0, G), :]
        return 0

    lax.fori_loop(0, T // G, group, 0)


def _peer_v_kernel(idx_s, w_ref, spread_ref, tab_ref, o_ref, stage):
    T = w_ref.shape[0]
    G = PEER_GROUP
    diag = _piece_diag()

    def group(gi, _):
        g0 = pl.multiple_of(gi * G, G)
        w8 = _dot(w_ref[pl.ds(g0, G), :].astype(bf16), spread_ref[...])
        for t in range(G):
            _gather_rows(idx_s, (g0 + t) * PEER_PAIRS, tab_ref, stage)
            wexp = jnp.where(diag, jnp.broadcast_to(w8[t:t + 1, :], diag.shape), 0.0).astype(bf16)
            r0 = pl.multiple_of((g0 + t) * SUBLANES, SUBLANES)
            o_ref[pl.ds(r0, SUBLANES), :] = _dot(wexp, pltpu.bitcast(stage[...], bf16))
        return 0

    lax.fori_loop(0, T // G, group, 0)


def _table_spec():
    return pl.BlockSpec((PEER_N * ROW_SUB, LANES), lambda i: (0, 0), pipeline_mode=pl.Buffered(1))


def _smem_flat(T):
    return pl.BlockSpec((T * PEER_PAIRS,), lambda i: (i,), memory_space=pltpu.SMEM)


def _pair_pieces():
    k = lax.broadcasted_iota(i32, (PEER_PAIRS, STAGE_COLS), 0)
    c = lax.broadcasted_iota(i32, (PEER_PAIRS, STAGE_COLS), 1)
    return (c // SUBLANES == k).astype(bf16)


def _peer_u_call(idx4, hf8, gates, u_pk, start):
    count = idx4.shape[0]
    T = PEER_TOKENS
    first = start // T
    return pl.pallas_call(
        _peer_u_kernel,
        grid=(count // T,),
        in_specs=[_smem_flat(T),
                  pl.BlockSpec((T * SUBLANES, LANES), lambda i: (i + first, 0)),
                  pl.BlockSpec((T, PEER_PAIRS), lambda i: (i, 0)),
                  pl.BlockSpec((STAGE_COLS, PEER_PAIRS), lambda i: (0, 0)), _table_spec()],
        out_specs=pl.BlockSpec((T, PEER_PAIRS), lambda i: (i, 0)),
        out_shape=jax.ShapeDtypeStruct((count, PEER_PAIRS), f32),
        scratch_shapes=[pltpu.VMEM((STAGE_ROWS, LANES), i32), pltpu.VMEM((PEER_GROUP // 2, STAGE_COLS), f32),
                        pltpu.VMEM((PEER_PAIRS * SUBLANES, LANES), f32),
                        pltpu.VMEM((PEER_GROUP // 2 * PEER_PAIRS, LANES), f32)],
        compiler_params=_params(("arbitrary",)),
        name="peer_u",
    )(idx4.reshape(-1), hf8, gates, _pair_pieces().T, u_pk)


def _peer_v_call(idx4, wgt, v_pk):
    count = wgt.shape[0]
    T = PEER_TOKENS
    return pl.pallas_call(
        _peer_v_kernel,
        grid=(count // T,),
        in_specs=[_smem_flat(T), pl.BlockSpec((T, PEER_PAIRS), lambda i: (i, 0)),
                  pl.BlockSpec((PEER_PAIRS, STAGE_COLS), lambda i: (0, 0)), _table_spec()],
        out_specs=pl.BlockSpec((T * SUBLANES, LANES), lambda i: (i, 0)),
        out_shape=jax.ShapeDtypeStruct((count * SUBLANES, LANES), f32),
        scratch_shapes=[pltpu.VMEM((STAGE_ROWS, LANES), i32)],
        compiler_params=_params(("arbitrary",)),
        name="peer_v",
    )(idx4.reshape(-1), wgt, _pair_pieces(), v_pk)


SC_CORES = 2
SC_SUBCORES = 16
SC_LANES = 16
SC_WORKERS = SC_CORES * SC_SUBCORES
SC_CHUNK = 16
SC_GROUP = 8
SC_PIECES = (7936, 9728, 11776, 12032, 10496)


def _sc_params():
    cp = pltpu.CompilerParams()
    if "needs_layout_passes" in pltpu.CompilerParams.__dataclass_fields__:
        cp = dataclasses.replace(cp, needs_layout_passes=False)
    return cp


def _peer_v_sc_call(idx, wgt, v3):
    n = idx.shape[0]
    per_worker = n // SC_WORKERS
    groups = LANES // SC_LANES
    n_chunks = PEER_PAIRS // SC_CHUNK
    mesh = plsc.VectorSubcoreMesh(core_axis_name="c", subcore_axis_name="s")
    rows_t = pltpu.VMEM((SC_CHUNK, SUBLANES, LANES), f32)

    @functools.partial(
        pl.kernel, mesh=mesh, compiler_params=_sc_params(),
        out_type=jax.ShapeDtypeStruct((n, SUBLANES, LANES), f32),
        scratch_types=[pltpu.VMEM((SC_GROUP, PEER_PAIRS), i32), pltpu.VMEM((SC_GROUP, PEER_PAIRS), f32),
                       rows_t, rows_t, pltpu.VMEM((SC_CHUNK, SC_LANES), f32),
                       pltpu.VMEM((SC_GROUP, SUBLANES, LANES), f32),
                       pltpu.SemaphoreType.DMA, pltpu.SemaphoreType.DMA])
    def sc_kernel(idx_hbm, w_hbm, tab_hbm, out_hbm, idx_v, w_v, rows_a, rows_b, wb_v, out_v, sem_a, sem_b):
        worker = lax.axis_index("s") * SC_CORES + lax.axis_index("c")
        zero = jnp.zeros((SC_LANES,), f32)
        bufs = ((rows_a, sem_a), (rows_b, sem_b))

        def gather(j, c):
            rows, sem = bufs[c % 2]
            return pltpu.make_async_copy(tab_hbm.at[idx_v.at[j, pl.ds(c * SC_CHUNK, SC_CHUNK)]], rows, sem)

        @pl.loop(0, per_worker // SC_GROUP)
        def _(g):
            t0 = worker * per_worker + g * SC_GROUP
            pltpu.sync_copy(idx_hbm.at[pl.ds(t0, SC_GROUP)], idx_v)
            pltpu.sync_copy(w_hbm.at[pl.ds(t0, SC_GROUP)], w_v)
            gather(0, 0).start()

            @pl.loop(0, SC_GROUP)
            def _(j):
                for c in range(n_chunks):
                    if c + 1 < n_chunks:
                        gather(j, c + 1).start()
                    else:
                        @pl.when(j + 1 < SC_GROUP)
                        def _():
                            gather(j + 1, 0).start()
                    gather(j, c).wait()
                    rows = bufs[c % 2][0]

                    @pl.loop(0, SC_CHUNK)
                    def _(r):
                        wb_v[r, :] = plsc.load_gather(
                            w_v, [jnp.full((SC_LANES,), 0, i32) + j, jnp.full((SC_LANES,), c * SC_CHUNK, i32) + r])

                    for s2 in range(SUBLANES // 2):
                        def where(a, s2=s2):
                            return 2 * s2 + a // groups, pl.ds((a % groups) * SC_LANES, SC_LANES)

                        def row_body(r, accs, rows=rows, where=where):
                            wv = wb_v[r, :]
                            return tuple(accs[a] + wv * rows[(r,) + where(a)] for a in range(2 * groups))

                        if c == 0:
                            init = (zero,) * (2 * groups)
                        else:
                            init = tuple(out_v[(j,) + where(a)] for a in range(2 * groups))
                        accs = lax.fori_loop(0, SC_CHUNK, row_body, init)
                        for a in range(2 * groups):
                            out_v[(j,) + where(a)] = accs[a]

            pltpu.sync_copy(out_v, out_hbm.at[pl.ds(t0, SC_GROUP)])

    return sc_kernel(idx, wgt, v3)


def _ln_kernel(x_ref, y_ref, gt_ref, g_ref, b_ref, o_ref):
    tm = x_ref.shape[0]
    y = jnp.concatenate([y_ref[pl.ds(r, tm, stride=SUBLANES), :] for r in range(SUBLANES)], axis=-1)
    o_ref[...] = _layer_norm(DN_ALPHA * x_ref[...] + y * gt_ref[...], g_ref[...], b_ref[...])


def _ln_call(x1, y8, mod6, lng, lnb, S):
    N = x1.shape[0]
    tm = 512
    row = pl.BlockSpec((tm, D_MODEL), lambda i: (i, 0))
    vec = pl.BlockSpec((1, D_MODEL), lambda i: (0, 0))
    return pl.pallas_call(
        _ln_kernel,
        grid=(N // tm,),
        in_specs=[row, pl.BlockSpec((tm * SUBLANES, LANES), lambda i: (i, 0)), _mod_spec(5, S // tm), vec, vec],
        out_specs=row,
        out_shape=jax.ShapeDtypeStruct((N, D_MODEL), f32),
        compiler_params=_params(("arbitrary",)),
        name="ln",
    )(x1, y8, mod6, lng, lnb)


def _pack_table(t):
    tb = lax.bitcast_convert_type(t.astype(bf16), jnp.uint16).astype(jnp.uint32)
    word = tb[:, :ROW_WORDS] | (tb[:, ROW_WORDS:] << 16)
    return lax.bitcast_convert_type(word, i32).reshape(PEER_N * ROW_SUB, LANES)


def _pad_keys(sub_keys_l):
    half = PEER_DKEY // 2
    z = jnp.zeros((PEER_HEADS, PEER_NKEYS, half), f32)
    k1 = jnp.concatenate([sub_keys_l[0], z], axis=-1).astype(bf16)
    k2 = jnp.concatenate([z, sub_keys_l[1]], axis=-1).astype(bf16)
    return k1, k2


def _layer(x, c, lb_logits, B, S, layer, p):
    (w_ada, b_ada, w_in, hgn, wua, wub, wo, wpq, sub_keys, pu, pv, ln_g, ln_b) = p
    mod6 = _mod_call(c, w_ada, b_ada.reshape(1, -1)).reshape(B, 6, 1, D_MODEL)
    proj = _proj_call(x, mod6, w_in.astype(bf16), S)
    oa = _hgrn_call(layer.reshape(1), proj, lb_logits, B, S)
    ob = _sb_call(proj, B, S)
    x1, hf8, pq = _merge_call(x, oa, proj, ob, mod6, hgn.reshape(1, -1), wua.astype(bf16),
                              wub.astype(bf16), wo.astype(bf16), wpq.astype(bf16),
                              ln_g[0:1], ln_b[0:1], S)
    k1p, k2p = _pad_keys(sub_keys)
    N = x.shape[0]
    u_pk = _pack_table(pu)
    v3 = pv.reshape(PEER_N, SUBLANES, LANES)
    pieces, start, after = [], 0, k1p
    for count in SC_PIECES + (N - sum(SC_PIECES),):
        idx_p, gates_p = _topk_call(pq, k1p, k2p, start, count, after)
        wgt = _peer_u_call(idx_p, hf8, gates_p, u_pk, start)
        if len(pieces) < len(SC_PIECES):
            y_p = _peer_v_sc_call(lax.shift_right_logical(idx_p, 2), wgt, v3).reshape(count * SUBLANES, LANES)
        else:
            y_p = _peer_v_call(idx_p, wgt, _pack_table(pv))
        pieces.append(y_p)
        start += count
        after = wgt
    y8 = jnp.concatenate(pieces, axis=0)
    return _ln_call(x1, y8, mod6, ln_g[1:2], ln_b[1:2], S)


def kernel(x, c, w_ada, b_ada, w_in, lb_logits, hg_norm_g, w_up_a, w_up_b, w_o, w_pq, sub_keys,
           peer_u, peer_v, ln_g, ln_b):
    B, S, _ = x.shape
    xs = (jnp.arange(DEPTH, dtype=i32), w_ada, b_ada, w_in, hg_norm_g, w_up_a, w_up_b, w_o, w_pq,
          sub_keys, peer_u, peer_v, ln_g, ln_b)

    def step(xc, per_layer):
        return _layer(xc, c, lb_logits, B, S, per_layer[0], per_layer[1:]), None

    out, _ = lax.scan(step, x.reshape(B * S, D_MODEL), xs)
    return out.reshape(B, S, D_MODEL)
```

```python
import dataclasses
import functools
import math

import jax
import jax.numpy as jnp
from jax import lax
from jax.experimental import pallas as pl
from jax.experimental.pallas import tpu as pltpu
from jax.experimental.pallas import tpu_sc as plsc

f32 = jnp.float32
bf16 = jnp.bfloat16
i32 = jnp.int32

D_MODEL = 1024
DEPTH = 4
HG_HEADS = 4
HG_D = 128
HG_W = HG_HEADS * HG_D
SB_HEADS = 8
SB_DH = 64
SB_W = SB_HEADS * SB_DH
SB_BLOCK = 128
PEER_HEADS = 8
PEER_NKEYS = 128
PEER_N = PEER_NKEYS * PEER_NKEYS
PEER_DKEY = 128
PEER_TOPK = 16
PEER_PAIRS = PEER_HEADS * PEER_TOPK
IN_WIDTH = 4 * HG_W + 3 * SB_W + 2 * D_MODEL
DN_ALPHA = (2.0 * DEPTH) ** 0.25
LN_EPS = 1e-5
RMS_EPS = 1e-6

LANES = 128
SUBLANES = 8
ROW_WORDS = D_MODEL // 2
ROW_SUB = ROW_WORDS // LANES
VMEM_LIMIT = 48 * 1024 * 1024

COL_QA, COL_FA, COL_IA = 0, 4, 8
COL_QB, COL_KB, COL_VB = 16, 20, 24
COL512_GA, COL512_GATE_A, COL512_GATE_B = 3, 7, 9


def _nt(a, b):
    return lax.dot_general(a, b, (((1,), (1,)), ((), ())), preferred_element_type=f32)


def _tn(a, b):
    return lax.dot_general(a, b, (((0,), (0,)), ((), ())), preferred_element_type=f32)


def _dot(a, b):
    return jnp.dot(a, b, preferred_element_type=f32)


def _split_bf16(a):
    hi = a.astype(bf16)
    lo = (a - hi.astype(f32)).astype(bf16)
    return hi, lo


def _params(sem):
    return pltpu.CompilerParams(dimension_semantics=sem, vmem_limit_bytes=VMEM_LIMIT)


def _mod_kernel(c_ref, w_ref, b_ref, o_ref):
    c = c_ref[...]
    cond = (c * jax.nn.sigmoid(c)).astype(bf16)
    o_ref[...] = _dot(cond, w_ref[...].astype(bf16)) + b_ref[...]


def _mod_call(c, w_ada_l, b_ada_l):
    B = c.shape[0]
    tn = 1536
    return pl.pallas_call(
        _mod_kernel,
        grid=(6 * D_MODEL // tn,),
        in_specs=[pl.BlockSpec((B, D_MODEL), lambda j: (0, 0)),
                  pl.BlockSpec((D_MODEL, tn), lambda j: (0, j)),
                  pl.BlockSpec((1, tn), lambda j: (0, j))],
        out_specs=pl.BlockSpec((B, tn), lambda j: (0, j)),
        out_shape=jax.ShapeDtypeStruct((B, 6 * D_MODEL), f32),
        compiler_params=_params(("arbitrary",)),
        name="mod",
    )(c, w_ada_l, b_ada_l)


def _mod_spec(which, tiles_per_seq):
    return pl.BlockSpec((None, None, 1, D_MODEL), lambda i: (i // tiles_per_seq, which, 0, 0))


def _proj_kernel(x_ref, sh_ref, sc_ref, w_ref, o_ref):
    hm = (x_ref[...] * (1.0 + sc_ref[...]) + sh_ref[...]).astype(bf16)
    for n0 in range(0, IN_WIDTH, 512):
        o_ref[:, n0:n0 + 512] = _dot(hm, w_ref[:, n0:n0 + 512])


def _proj_call(x, mod6, w_in_bf, S):
    N = x.shape[0]
    tm = 256
    tps = S // tm
    return pl.pallas_call(
        _proj_kernel,
        grid=(N // tm,),
        in_specs=[pl.BlockSpec((tm, D_MODEL), lambda i: (i, 0)),
                  _mod_spec(0, tps), _mod_spec(1, tps),
                  pl.BlockSpec((D_MODEL, IN_WIDTH), lambda i: (0, 0))],
        out_specs=pl.BlockSpec((tm, IN_WIDTH), lambda i: (i, 0)),
        out_shape=jax.ShapeDtypeStruct((N, IN_WIDTH), f32),
        compiler_params=_params(("arbitrary",)),
        name="proj",
    )(x, mod6, mod6, w_in_bf)


HG_CHUNK = 128
HG_SUB = 16
HG_STEP_ROWS = 1024


def _hgrn_chunk(q, z, v, lb, st, tri):
    one_m_lb = 1.0 - lb
    g = jnp.log(lb + one_m_lb * jax.nn.sigmoid(z))
    k = one_m_lb * jax.nn.sigmoid(-z)
    g_hi, g_lo = _split_bf16(g)
    b = _dot(tri, g_hi) + _dot(tri, g_lo)
    b_end = b[HG_CHUNK - 1:HG_CHUNK, :]
    inter = _nt((q * jnp.exp(b)).astype(bf16), st.astype(bf16))
    v_bf = v.astype(bf16)
    row_c = lax.broadcasted_iota(i32, (HG_CHUNK, HG_D), 0)
    row_s = lax.broadcasted_iota(i32, (HG_SUB, HG_D), 0)
    blocks = []
    for sub in range(HG_CHUNK // HG_SUB):
        r0 = sub * HG_SUB
        bs = b[r0:r0 + HG_SUB]
        qs = q[r0:r0 + HG_SUB]
        ks = k[r0:r0 + HG_SUB]
        vs = v[r0:r0 + HG_SUB]
        rows = []
        for t in range(HG_SUB):
            m = row_s <= t
            e = jnp.where(m, jnp.exp(jnp.where(m, bs[t:t + 1] - bs, 0.0)), 0.0)
            p = (qs[t:t + 1] * ks) * e
            srow = jnp.sum(p, axis=-1, keepdims=True)
            rows.append(jnp.sum(srow * vs, axis=0, keepdims=True))
        o_sub = jnp.concatenate(rows, axis=0)
        if sub > 0:
            bref = b[r0 - 1:r0, :]
            qi = (qs * jnp.exp(bs - bref)).astype(bf16)
            past = row_c < r0
            ki = jnp.where(past, k * jnp.exp(jnp.where(past, bref - b, 0.0)), 0.0).astype(bf16)
            o_sub = o_sub + _dot(_nt(qi, ki).astype(bf16), v_bf)
        blocks.append(o_sub)
    intra = jnp.concatenate(blocks, axis=0)
    kd = (k * jnp.exp(b_end - b)).astype(bf16)
    st_new = st * jnp.exp(b_end) + _tn(v_bf, kd)
    return inter + intra, st_new


def _hgrn_kernel(l_ref, q_ref, z_ref, v_ref, lbl_ref, o_ref, st_ref):
    @pl.when(pl.program_id(2) == 0)
    def _():
        st_ref[...] = jnp.zeros_like(st_ref)

    logits = lbl_ref[...]
    e = jnp.exp(logits - jnp.max(logits, axis=0, keepdims=True))
    p = e / jnp.sum(e, axis=0, keepdims=True)
    rid = lax.broadcasted_iota(i32, p.shape, 0)
    l = l_ref[0]
    lb = jnp.sum(jnp.where((rid >= 1) & (rid <= l), p, 0.0), axis=0, keepdims=True)

    r = lax.broadcasted_iota(i32, (HG_CHUNK, HG_CHUNK), 0)
    c = lax.broadcasted_iota(i32, (HG_CHUNK, HG_CHUNK), 1)
    tri = jnp.where(c <= r, 1.0, 0.0).astype(bf16)

    def body(ci, st):
        r0 = pl.multiple_of(ci * HG_CHUNK, HG_CHUNK)
        out, st = _hgrn_chunk(q_ref[pl.ds(r0, HG_CHUNK), :], z_ref[pl.ds(r0, HG_CHUNK), :],
                              v_ref[pl.ds(r0, HG_CHUNK), :], lb, st, tri)
        o_ref[pl.ds(r0, HG_CHUNK), :] = out
        return st

    st_ref[...] = lax.fori_loop(0, HG_STEP_ROWS // HG_CHUNK, body, st_ref[...])


def _hgrn_call(layer, proj, lb_logits, B, S):
    N = proj.shape[0]
    R = HG_STEP_ROWS
    spb = S // R

    def col(c0):
        return pl.BlockSpec((R, HG_D), lambda b, h, s, l: (b * spb + s, c0 + h))

    grid_spec = pltpu.PrefetchScalarGridSpec(
        num_scalar_prefetch=1,
        grid=(B, HG_HEADS, spb),
        in_specs=[col(COL_QA), col(COL_FA), col(COL_IA),
                  pl.BlockSpec((DEPTH, HG_D), lambda b, h, s, l: (0, h))],
        out_specs=pl.BlockSpec((R, HG_D), lambda b, h, s, l: (b * spb + s, h)),
        scratch_shapes=[pltpu.VMEM((HG_D, HG_D), f32)],
    )
    return pl.pallas_call(
        _hgrn_kernel,
        grid_spec=grid_spec,
        out_shape=jax.ShapeDtypeStruct((N, HG_W), f32),
        compiler_params=_params(("arbitrary", "arbitrary", "arbitrary")),
        name="hgrn",
    )(layer, proj, proj, proj, lb_logits)


SB_SCALE = 1.0 / math.sqrt(SB_DH)
SB_PAIRS = 4
SB_GROUP = 3
SB_DEAD = -104.0
SB_NEVER = -(1 << 20)


def _sb_kernel(q_ref, k_ref, v_ref, o_ref):
    i = pl.program_id(2)
    T = SB_BLOCK
    P = SB_PAIRS
    lane = lax.broadcasted_iota(i32, (T, T), 1)
    lane2 = lax.broadcasted_iota(i32, (T, 2 * T), 1)
    row2 = lax.broadcasted_iota(i32, (T, 2 * T), 0)
    u = jnp.where((row2 > lane2) | (lane2 >= T), 1.0, 0.0).astype(bf16)
    qms = []
    for p in range(P):
        q2 = q_ref[:, p * T:(p + 1) * T] * SB_SCALE
        qms.append(jnp.concatenate([jnp.where(lane < SB_DH, q2, 0.0), jnp.where(lane >= SB_DH, q2, 0.0)],
                                   axis=0).astype(bf16))
    lane_s = lax.broadcasted_iota(i32, (2 * T, T), 1)
    row_s = lax.broadcasted_iota(i32, (2 * T, T), 0)
    key_minus_query = lane_s - (row_s & (T - 1))

    def trip(state):
        j_hi, carries, accs = state
        carries, accs = list(carries), list(accs)
        chains = [(p, b) for p in range(P) for b in range(SB_GROUP)]
        allowed, ks, vs = [], {}, {}
        for b in range(SB_GROUP):
            j = j_hi - b
            r0 = pl.multiple_of(jnp.maximum(j, 0) * T, T)
            allowed.append(key_minus_query < jnp.where(j >= 0, (i - j) * T, SB_NEVER))
            for p in range(P):
                ks[p, b] = k_ref[pl.ds(r0, T), p * T:(p + 1) * T].astype(bf16)
                vs[p, b] = v_ref[pl.ds(r0, T), p * T:(p + 1) * T].astype(bf16)
        zs = {c: _nt(qms[c[0]], ks[c]) for c in chains}
        lss, his, los = {}, {}, {}
        for c in chains:
            z = zs[c]
            lss[c] = jnp.minimum(z, 0.0) - jnp.log(1.0 + jnp.exp(-jnp.abs(z)))
            his[c], los[c] = _split_bf16(jnp.where(allowed[c[1]], lss[c] - z, 0.0))
        rs = {c: _dot(his[c], u) + _dot(los[c], u) for c in chains}
        ws = {}
        for p, b in chains:
            r = rs[p, b]
            ws[p, b] = jnp.where(allowed[b], jnp.exp(lss[p, b] + r[:, :T] + carries[p]), 0.0).astype(bf16)
            carries[p] = carries[p] + r[:, T:]
        for p, b in chains:
            accs[p] = accs[p] + _dot(ws[p, b], vs[p, b])
        return j_hi - SB_GROUP, tuple(carries), tuple(accs)

    def live(state):
        j_hi, carries, _ = state
        top = carries[0]
        for c in carries[1:]:
            top = jnp.maximum(top, c)
        return (j_hi >= 0) & (jnp.max(top) > SB_DEAD)

    zero = jnp.zeros((2 * T, T), f32)
    _, _, accs = lax.while_loop(live, trip, (i, (zero,) * P, (zero,) * P))
    for p in range(P):
        o_ref[:, p * T:(p + 1) * T] = jnp.where(lane < SB_DH, accs[p][:T], accs[p][T:])


def _sb_call(proj, B, S):
    N = proj.shape[0]
    nq = S // SB_BLOCK
    W = SB_PAIRS * LANES
    return pl.pallas_call(
        _sb_kernel,
        grid=(B, SB_HEADS // (2 * SB_PAIRS), nq),
        in_specs=[pl.BlockSpec((SB_BLOCK, W), lambda b, p, i: (b * nq + i, COL_QB // SB_PAIRS + p)),
                  pl.BlockSpec((S, W), lambda b, p, i: (b, COL_KB // SB_PAIRS + p)),
                  pl.BlockSpec((S, W), lambda b, p, i: (b, COL_VB // SB_PAIRS + p))],
        out_specs=pl.BlockSpec((SB_BLOCK, W), lambda b, p, i: (b * nq + i, p)),
        out_shape=jax.ShapeDtypeStruct((N, SB_W), f32),
        compiler_params=_params(("arbitrary", "arbitrary", "arbitrary")),
        name="sb",
    )(proj, proj, proj)


def _layer_norm(r, g, b):
    mu = jnp.mean(r, axis=-1, keepdims=True)
    d = r - mu
    var = jnp.mean(d * d, axis=-1, keepdims=True)
    return d * lax.rsqrt(var + LN_EPS) * g + b


def _merge_kernel(x_ref, oa_ref, ga_ref, ob_ref, gta0_ref, gta1_ref, gtb0_ref, gtb1_ref,
                  gt1_ref, sh2_ref, sc2_ref, hgn_ref, wua_ref, wub_ref, wo_ref, wpq_ref,
                  lng_ref, lnb_ref, x1_ref, hf_ref, pq_ref):
    oa = oa_ref[...]
    hgn = hgn_ref[...]
    segs = []
    for h in range(HG_HEADS):
        seg = oa[:, h * HG_D:(h + 1) * HG_D]
        ms = jnp.mean(seg * seg, axis=-1, keepdims=True)
        segs.append(seg * lax.rsqrt(ms + RMS_EPS) * hgn[:, h * HG_D:(h + 1) * HG_D])
    ga = ga_ref[...]
    oa_n = jnp.concatenate(segs, axis=-1) * (ga * jax.nn.sigmoid(ga))
    ma = _dot(oa_n.astype(bf16), wua_ref[...])
    mb = _dot(ob_ref[...].astype(bf16), wub_ref[...])
    gate_a = jnp.concatenate([gta0_ref[...], gta1_ref[...]], axis=-1)
    gate_b = jnp.concatenate([gtb0_ref[...], gtb1_ref[...]], axis=-1)
    merged = jax.nn.sigmoid(gate_a) * ma + jax.nn.sigmoid(gate_b) * mb
    y = _dot(merged.astype(bf16), wo_ref[...]) * gt1_ref[...]
    x1 = _layer_norm(DN_ALPHA * x_ref[...] + y, lng_ref[...], lnb_ref[...])
    x1_ref[...] = x1
    hf = x1 * (1.0 + sc2_ref[...]) + sh2_ref[...]
    tm = hf.shape[0]
    for r in range(SUBLANES):
        hf_ref[pl.ds(r, tm, stride=SUBLANES), :] = hf[:, r * LANES:(r + 1) * LANES]
    pq_ref[...] = _dot(hf.astype(bf16), wpq_ref[...])


def _merge_call(x, oa, proj, ob, mod6, hgn, wua, wub, wo, wpq, lng, lnb, S):
    N = x.shape[0]
    tm = 256
    tps = S // tm

    def full(shape):
        return pl.BlockSpec(shape, lambda i: (0,) * len(shape))

    def p512(c):
        return pl.BlockSpec((tm, 512), lambda i: (i, c))

    row = pl.BlockSpec((tm, D_MODEL), lambda i: (i, 0))
    out = jax.ShapeDtypeStruct((N, D_MODEL), f32)
    return pl.pallas_call(
        _merge_kernel,
        grid=(N // tm,),
        in_specs=[row, pl.BlockSpec((tm, HG_W), lambda i: (i, 0)), p512(COL512_GA),
                  pl.BlockSpec((tm, SB_W), lambda i: (i, 0)),
                  p512(COL512_GATE_A), p512(COL512_GATE_A + 1),
                  p512(COL512_GATE_B), p512(COL512_GATE_B + 1),
                  _mod_spec(2, tps), _mod_spec(3, tps), _mod_spec(4, tps),
                  full((1, HG_W)), full((HG_W, D_MODEL)), full((SB_W, D_MODEL)),
                  full((D_MODEL, D_MODEL)), full((D_MODEL, D_MODEL)),
                  full((1, D_MODEL)), full((1, D_MODEL))],
        out_specs=[row, pl.BlockSpec((tm * SUBLANES, LANES), lambda i: (i, 0)), row],
        out_shape=[out, jax.ShapeDtypeStruct((N * SUBLANES, LANES), f32), out],
        compiler_params=_params(("arbitrary",)),
        name="merge",
    )(x, oa, proj, ob, proj, proj, proj, proj, mod6, mod6, mod6, hgn, wua, wub, wo, wpq, lng, lnb)


TOPK_TOKENS = 256


def _top16(s, payload=None):
    R = s.shape[0]
    rid = lax.broadcasted_iota(i32, s.shape, 0).astype(f32)
    vals, ids = [], []
    for _ in range(PEER_TOPK):
        m = jnp.max(s, axis=0, keepdims=True)
        first = jnp.min(jnp.where(s == m, rid, float(R)), axis=0, keepdims=True)
        sel = rid == first
        vals.append(m)
        if payload is None:
            ids.append(first.astype(i32))
        else:
            ids.append(jnp.max(jnp.where(sel, payload, -1), axis=0, keepdims=True))
        s = jnp.where(sel, -jnp.inf, s)
    return vals, ids


def _topk_kernel(pq_ref, k1_ref, k2_ref, after_ref, idx_ref, gate_ref, idx_t, gate_t):
    del after_ref
    H = SUBLANES

    def halves(h):
        c0 = pl.multiple_of(h * PEER_DKEY, PEER_DKEY)
        qh = pq_ref[:, pl.ds(c0, PEER_DKEY)].astype(bf16)
        v1, i1 = _top16(_nt(k1_ref[h], qh))
        v2, i2 = _top16(_nt(k2_ref[h], qh))
        return tuple(jnp.concatenate(a, axis=0) for a in (v1, i1, v2, i2))

    def combine(h, tops):
        v1a, i1a, v2a, i2a = tops
        v1 = [v1a[a:a + 1] for a in range(H)]
        i1 = [i1a[a:a + 1] for a in range(H)]
        cand = ([v1[0] + v2a[:H], v1[0] + v2a[H:]] + [v1[a] + v2a[:H] for a in range(1, H)]
                + [v1a[H:] + v2a[0:1]])
        cidx = ([i1[0] * PEER_NKEYS + i2a[:H], i1[0] * PEER_NKEYS + i2a[H:]]
                + [i1[a] * PEER_NKEYS + i2a[:H] for a in range(1, H)] + [i1a[H:] * PEER_NKEYS + i2a[0:1]])
        tv, ti = _top16(jnp.concatenate(cand, axis=0), jnp.concatenate(cidx, axis=0) * ROW_SUB)
        tva = jnp.concatenate(tv, axis=0)
        e = jnp.exp(tva - tv[0])
        r0 = pl.multiple_of(h * PEER_TOPK, PEER_TOPK)
        gate_t[pl.ds(r0, PEER_TOPK), :] = e / jnp.sum(e, axis=0, keepdims=True)
        idx_t[pl.ds(r0, PEER_TOPK), :] = jnp.concatenate(ti, axis=0)

    def step(h, tops):
        nxt = halves(h + 1)
        combine(h, tops)
        return nxt

    last = lax.fori_loop(0, PEER_HEADS - 1, step, halves(0))
    combine(PEER_HEADS - 1, last)
    idx_ref[...] = idx_t[...].T
    gate_ref[...] = gate_t[...].T


def _topk_call(pq, k1p, k2p, start, count, after):
    T = TOPK_TOKENS
    first = start // T
    keys = pl.BlockSpec((PEER_HEADS, PEER_NKEYS, PEER_DKEY), lambda i: (0, 0, 0))
    return pl.pallas_call(
        _topk_kernel,
        grid=(count // T,),
        in_specs=[pl.BlockSpec((T, D_MODEL), lambda i: (i + first, 0)), keys, keys,
                  pl.BlockSpec(memory_space=pl.ANY)],
        out_specs=[pl.BlockSpec((T, PEER_PAIRS), lambda i: (i, 0)),
                   pl.BlockSpec((T, PEER_PAIRS), lambda i: (i, 0))],
        out_shape=[jax.ShapeDtypeStruct((count, PEER_PAIRS), i32),
                   jax.ShapeDtypeStruct((count, PEER_PAIRS), f32)],
        scratch_shapes=[pltpu.VMEM((PEER_PAIRS, T), i32), pltpu.VMEM((PEER_PAIRS, T), f32)],
        compiler_params=_params(("arbitrary",)),
        name="topk",
    )(pq, k1p, k2p, after)


PEER_TOKENS = 128
PEER_GROUP = SUBLANES
STAGE_ROWS = PEER_PAIRS * ROW_SUB
STAGE_COLS = 2 * STAGE_ROWS
INV_SQRT2 = 1.0 / math.sqrt(2.0)


def _gather_rows(idx_s, k0, tab_ref, stage_ref):
    for k in range(PEER_PAIRS):
        e4 = pl.multiple_of(idx_s[k0 + k], ROW_SUB)
        stage_ref[k * ROW_SUB:(k + 1) * ROW_SUB, :] = tab_ref[pl.ds(e4, ROW_SUB), :]


def _piece_diag():
    sub = lax.broadcasted_iota(i32, (SUBLANES, STAGE_COLS), 0)
    col = lax.broadcasted_iota(i32, (SUBLANES, STAGE_COLS), 1)
    return (col & (SUBLANES - 1)) == 2 * (sub & (ROW_SUB - 1)) + lax.shift_right_logical(sub, 2)


def _peer_u_kernel(idx_s, hf_ref, g_ref, fold_ref, tab_ref, o_ref, stage, zs_ref):
    T = o_ref.shape[0]
    G = PEER_GROUP
    diag = _piece_diag()

    def group(gi, _):
        g0 = pl.multiple_of(gi * G, G)
        for t in range(G):
            _gather_rows(idx_s, (g0 + t) * PEER_PAIRS, tab_ref, stage)
            h_hi, h_lo = _split_bf16(hf_ref[pl.ds(pl.multiple_of((g0 + t) * SUBLANES, SUBLANES), SUBLANES), :])
            z = _nt(jnp.concatenate([h_hi, h_lo], axis=0), pltpu.bitcast(stage[...], bf16))
            zs_ref[t:t + 1, :] = jnp.sum(jnp.where(diag, z[:SUBLANES] + z[SUBLANES:], 0.0), axis=0, keepdims=True)
        zs_hi, zs_lo = _split_bf16(zs_ref[...])
        dots = _dot(zs_hi, fold_ref[...]) + _dot(zs_lo, fold_ref[...])
        act = 0.5 * dots * (1.0 + lax.erf(dots * INV_SQRT2))
        o_ref[pl.ds(g0, G), :] = act * g_ref[pl.ds(g0, G), :]
        return 0

    lax.fori_loop(0, T // G, group, 0)


def _peer_v_kernel(idx_s, w_ref, spread_ref, tab_ref, o_ref, stage):
    T = w_ref.shape[0]
    G = PEER_GROUP
    diag = _piece_diag()

    def group(gi, _):
        g0 = pl.multiple_of(gi * G, G)
        w8 = _dot(w_ref[pl.ds(g0, G), :].astype(bf16), spread_ref[...])
        for t in range(G):
            _gather_rows(idx_s, (g0 + t) * PEER_PAIRS, tab_ref, stage)
            wexp = jnp.where(diag, jnp.broadcast_to(w8[t:t + 1, :], diag.shape), 0.0).astype(bf16)
            r0 = pl.multiple_of((g0 + t) * SUBLANES, SUBLANES)
            o_ref[pl.ds(r0, SUBLANES), :] = _dot(wexp, pltpu.bitcast(stage[...], bf16))
        return 0

    lax.fori_loop(0, T // G, group, 0)


def _table_spec():
    return pl.BlockSpec((PEER_N * ROW_SUB, LANES), lambda i: (0, 0), pipeline_mode=pl.Buffered(1))


def _smem_flat(T):
    return pl.BlockSpec((T * PEER_PAIRS,), lambda i: (i,), memory_space=pltpu.SMEM)


def _pair_pieces():
    k = lax.broadcasted_iota(i32, (PEER_PAIRS, STAGE_COLS), 0)
    c = lax.broadcasted_iota(i32, (PEER_PAIRS, STAGE_COLS), 1)
    return (c // SUBLANES == k).astype(bf16)


def _peer_u_call(idx4, hf8, gates, u_pk, start):
    count = idx4.shape[0]
    T = PEER_TOKENS
    first = start // T
    return pl.pallas_call(
        _peer_u_kernel,
        grid=(count // T,),
        in_specs=[_smem_flat(T),
                  pl.BlockSpec((T * SUBLANES, LANES), lambda i: (i + first, 0)),
                  pl.BlockSpec((T, PEER_PAIRS), lambda i: (i, 0)),
                  pl.BlockSpec((STAGE_COLS, PEER_PAIRS), lambda i: (0, 0)), _table_spec()],
        out_specs=pl.BlockSpec((T, PEER_PAIRS), lambda i: (i, 0)),
        out_shape=jax.ShapeDtypeStruct((count, PEER_PAIRS), f32),
        scratch_shapes=[pltpu.VMEM((STAGE_ROWS, LANES), i32), pltpu.VMEM((PEER_GROUP, STAGE_COLS), f32)],
        compiler_params=_params(("arbitrary",)),
        name="peer_u",
    )(idx4.reshape(-1), hf8, gates, _pair_pieces().T, u_pk)


def _peer_v_call(idx4, wgt, v_pk):
    count = wgt.shape[0]
    T = PEER_TOKENS
    return pl.pallas_call(
        _peer_v_kernel,
        grid=(count // T,),
        in_specs=[_smem_flat(T), pl.BlockSpec((T, PEER_PAIRS), lambda i: (i, 0)),
                  pl.BlockSpec((PEER_PAIRS, STAGE_COLS), lambda i: (0, 0)), _table_spec()],
        out_specs=pl.BlockSpec((T * SUBLANES, LANES), lambda i: (i, 0)),
        out_shape=jax.ShapeDtypeStruct((count * SUBLANES, LANES), f32),
        scratch_shapes=[pltpu.VMEM((STAGE_ROWS, LANES), i32)],
        compiler_params=_params(("arbitrary",)),
        name="peer_v",
    )(idx4.reshape(-1), wgt, _pair_pieces(), v_pk)


SC_CORES = 2
SC_SUBCORES = 16
SC_LANES = 16
SC_WORKERS = SC_CORES * SC_SUBCORES
SC_CHUNK = 16
SC_GROUP = 8
SC_PIECES = (7936, 9728, 11776, 12032, 11264)


def _sc_params():
    cp = pltpu.CompilerParams()
    if "needs_layout_passes" in pltpu.CompilerParams.__dataclass_fields__:
        cp = dataclasses.replace(cp, needs_layout_passes=False)
    return cp


def _peer_v_sc_call(idx, wgt, v3):
    n = idx.shape[0]
    per_worker = n // SC_WORKERS
    groups = LANES // SC_LANES
    n_chunks = PEER_PAIRS // SC_CHUNK
    mesh = plsc.VectorSubcoreMesh(core_axis_name="c", subcore_axis_name="s")
    rows_t = pltpu.VMEM((SC_CHUNK, SUBLANES, LANES), f32)

    @functools.partial(
        pl.kernel, mesh=mesh, compiler_params=_sc_params(),
        out_type=jax.ShapeDtypeStruct((n, SUBLANES, LANES), f32),
        scratch_types=[pltpu.VMEM((SC_GROUP, PEER_PAIRS), i32), pltpu.VMEM((SC_GROUP, PEER_PAIRS), f32),
                       rows_t, rows_t, pltpu.VMEM((SC_CHUNK, SC_LANES), f32),
                       pltpu.VMEM((SC_GROUP, SUBLANES, LANES), f32),
                       pltpu.SemaphoreType.DMA, pltpu.SemaphoreType.DMA])
    def sc_kernel(idx_hbm, w_hbm, tab_hbm, out_hbm, idx_v, w_v, rows_a, rows_b, wb_v, out_v, sem_a, sem_b):
        worker = lax.axis_index("s") * SC_CORES + lax.axis_index("c")
        zero = jnp.zeros((SC_LANES,), f32)
        bufs = ((rows_a, sem_a), (rows_b, sem_b))

        def gather(j, c):
            rows, sem = bufs[c % 2]
            return pltpu.make_async_copy(tab_hbm.at[idx_v.at[j, pl.ds(c * SC_CHUNK, SC_CHUNK)]], rows, sem)

        @pl.loop(0, per_worker // SC_GROUP)
        def _(g):
            t0 = worker * per_worker + g * SC_GROUP
            pltpu.sync_copy(idx_hbm.at[pl.ds(t0, SC_GROUP)], idx_v)
            pltpu.sync_copy(w_hbm.at[pl.ds(t0, SC_GROUP)], w_v)
            gather(0, 0).start()

            @pl.loop(0, SC_GROUP)
            def _(j):
                for c in range(n_chunks):
                    if c + 1 < n_chunks:
                        gather(j, c + 1).start()
                    else:
                        @pl.when(j + 1 < SC_GROUP)
                        def _():
                            gather(j + 1, 0).start()
                    gather(j, c).wait()
                    rows = bufs[c % 2][0]

                    @pl.loop(0, SC_CHUNK)
                    def _(r):
                        wb_v[r, :] = plsc.load_gather(
                            w_v, [jnp.full((SC_LANES,), 0, i32) + j, jnp.full((SC_LANES,), c * SC_CHUNK, i32) + r])

                    for s2 in range(SUBLANES // 2):
                        def where(a, s2=s2):
                            return 2 * s2 + a // groups, pl.ds((a % groups) * SC_LANES, SC_LANES)

                        def row_body(r, accs, rows=rows, where=where):
                            wv = wb_v[r, :]
                            return tuple(accs[a] + wv * rows[(r,) + where(a)] for a in range(2 * groups))

                        if c == 0:
                            init = (zero,) * (2 * groups)
                        else:
                            init = tuple(out_v[(j,) + where(a)] for a in range(2 * groups))
                        accs = lax.fori_loop(0, SC_CHUNK, row_body, init)
                        for a in range(2 * groups):
                            out_v[(j,) + where(a)] = accs[a]

            pltpu.sync_copy(out_v, out_hbm.at[pl.ds(t0, SC_GROUP)])

    return sc_kernel(idx, wgt, v3)


def _ln_kernel(x_ref, y_ref, gt_ref, g_ref, b_ref, o_ref):
    tm = x_ref.shape[0]
    y = jnp.concatenate([y_ref[pl.ds(r, tm, stride=SUBLANES), :] for r in range(SUBLANES)], axis=-1)
    o_ref[...] = _layer_norm(DN_ALPHA * x_ref[...] + y * gt_ref[...], g_ref[...], b_ref[...])


def _ln_call(x1, y8, mod6, lng, lnb, S):
    N = x1.shape[0]
    tm = 512
    row = pl.BlockSpec((tm, D_MODEL), lambda i: (i, 0))
    vec = pl.BlockSpec((1, D_MODEL), lambda i: (0, 0))
    return pl.pallas_call(
        _ln_kernel,
        grid=(N // tm,),
        in_specs=[row, pl.BlockSpec((tm * SUBLANES, LANES), lambda i: (i, 0)), _mod_spec(5, S // tm), vec, vec],
        out_specs=row,
        out_shape=jax.ShapeDtypeStruct((N, D_MODEL), f32),
        compiler_params=_params(("arbitrary",)),
        name="ln",
    )(x1, y8, mod6, lng, lnb)


def _pack_table(t):
    tb = lax.bitcast_convert_type(t.astype(bf16), jnp.uint16).astype(jnp.uint32)
    word = tb[:, :ROW_WORDS] | (tb[:, ROW_WORDS:] << 16)
    return lax.bitcast_convert_type(word, i32).reshape(PEER_N * ROW_SUB, LANES)


def _pad_keys(sub_keys_l):
    half = PEER_DKEY // 2
    z = jnp.zeros((PEER_HEADS, PEER_NKEYS, half), f32)
    k1 = jnp.concatenate([sub_keys_l[0], z], axis=-1).astype(bf16)
    k2 = jnp.concatenate([z, sub_keys_l[1]], axis=-1).astype(bf16)
    return k1, k2


def _layer(x, c, lb_logits, B, S, layer, p):
    (w_ada, b_ada, w_in, hgn, wua, wub, wo, wpq, sub_keys, pu, pv, ln_g, ln_b) = p
    mod6 = _mod_call(c, w_ada, b_ada.reshape(1, -1)).reshape(B, 6, 1, D_MODEL)
    proj = _proj_call(x, mod6, w_in.astype(bf16), S)
    oa = _hgrn_call(layer.reshape(1), proj, lb_logits, B, S)
    ob = _sb_call(proj, B, S)
    x1, hf8, pq = _merge_call(x, oa, proj, ob, mod6, hgn.reshape(1, -1), wua.astype(bf16),
                              wub.astype(bf16), wo.astype(bf16), wpq.astype(bf16),
                              ln_g[0:1], ln_b[0:1], S)
    k1p, k2p = _pad_keys(sub_keys)
    N = x.shape[0]
    u_pk = _pack_table(pu)
    v3 = pv.reshape(PEER_N, SUBLANES, LANES)
    pieces, start, after = [], 0, k1p
    for count in SC_PIECES + (N - sum(SC_PIECES),):
        idx_p, gates_p = _topk_call(pq, k1p, k2p, start, count, after)
        wgt = _peer_u_call(idx_p, hf8, gates_p, u_pk, start)
        if len(pieces) < len(SC_PIECES):
            y_p = _peer_v_sc_call(lax.shift_right_logical(idx_p, 2), wgt, v3).reshape(count * SUBLANES, LANES)
        else:
            y_p = _peer_v_call(idx_p, wgt, _pack_table(pv))
        pieces.append(y_p)
        start += count
        after = wgt
    y8 = jnp.concatenate(pieces, axis=0)
    return _ln_call(x1, y8, mod6, ln_g[1:2], ln_b[1:2], S)


def kernel(x, c, w_ada, b_ada, w_in, lb_logits, hg_norm_g, w_up_a, w_up_b, w_o, w_pq, sub_keys,
           peer_u, peer_v, ln_g, ln_b):
    B, S, _ = x.shape
    xs = (jnp.arange(DEPTH, dtype=i32), w_ada, b_ada, w_in, hg_norm_g, w_up_a, w_up_b, w_o, w_pq,
          sub_keys, peer_u, peer_v, ln_g, ln_b)

    def step(xc, per_layer):
        return _layer(xc, c, lb_logits, B, S, per_layer[0], per_layer[1:]), None

    out, _ = lax.scan(step, x.reshape(B * S, D_MODEL), xs)
    return out.reshape(B, S, D_MODEL)
```
